```python
import math
import jax
import jax.numpy as jnp
from jax import lax
import numpy as np

D_MODEL = 4096
BATCH = 1
SEQ = 8192
DEPTH = 4
DEC_BATCH = 2
DEC_SEQ = 8192
PAST_LEN = 128

GRID_W = 64
D_FF = 11008
N_SUB = 3
N_MOD = 3 * N_SUB
N_HYENA = (DEPTH + 1) // 2
N_NA = DEPTH // 2
HY_CONV = 3
HY_EMB = 33
HY_BANDS = (HY_EMB - 1) // 2
HY_HIDDEN = 64
HY_TARGET = 1e-2
HY_DECAY_PCT_SHORT = 0.3
HY_DECAY_PCT_LONG = 1.5
HY_MAX_DECAY = math.log(HY_TARGET) / HY_DECAY_PCT_SHORT
HY_MIN_DECAY = math.log(HY_TARGET) / HY_DECAY_PCT_LONG
HY_FILTER_SCALE = 0.02
NA_HEADS = 32
NA_HEAD_DIM = D_MODEL // NA_HEADS
NA_KH = 8
NA_KW = 16
EPS = 1e-6

kernel_name = 'hybrid_hyena_natten_macaron_encoder'

F32 = jnp.float32


def _rms_norm(x, g):
    xf = x.astype(F32)
    y = xf * lax.rsqrt(jnp.mean(xf * xf, axis=-1, keepdims=True) + EPS)
    return (y * g.astype(F32)).astype(x.dtype)


def _ada_pre(x, g, mod, j):
    h = _rms_norm(x, g)
    return h * (1 + mod[:, 3 * j + 1]) + mod[:, 3 * j]


def _swiglu(h, w_gate, w_up, w_down):
    return (jax.nn.silu(h @ w_gate) * (h @ w_up)) @ w_down


def _hyena_filter(L, w0, b0, w1, b1, w2, b2, freq, w_out):
    t = jnp.linspace(0.0, 1.0, L, dtype=F32)[:, None]
    w = (2.0 * math.pi / L) * jnp.arange(L, dtype=F32)[:, None]
    f = jnp.linspace(1e-4, HY_BANDS - 1, HY_BANDS, dtype=F32)[None, :]
    ang = w * f
    z = jnp.concatenate([t, jnp.cos(ang), -jnp.sin(ang)], axis=-1)
    fr = freq.astype(F32)
    hdn = jnp.sin(fr * (z @ w0.astype(F32) + b0.astype(F32)))
    hdn = jnp.sin(fr * (hdn @ w1.astype(F32) + b1.astype(F32)))
    hdn = jnp.sin(fr * (hdn @ w2.astype(F32) + b2.astype(F32)))
    taps = hdn @ w_out.astype(F32)
    deltas = jnp.abs(jnp.linspace(HY_MIN_DECAY, HY_MAX_DECAY, D_MODEL, dtype=F32))
    decay = jnp.exp(-t * deltas[None, :])
    h_fwd = taps[:, :D_MODEL] * decay
    h_bwd = taps[:, D_MODEL:] * decay
    return jnp.concatenate([h_fwd, jnp.zeros((1, D_MODEL), F32), h_bwd[:0:-1]], axis=0)


def _hyena(h, w_in, b_in, conv_w, f_w0, f_b0, f_w1, f_b1, f_w2, f_b2, f_freq, f_out,
           d_skip, w_out, b_out):
    B, L, D = h.shape
    u = h @ w_in + b_in
    up = jnp.pad(u, ((0, 0), (1, 1), (0, 0)))
    u = up[:, :-2] * conv_w[0] + up[:, 1:-1] * conv_w[1] + up[:, 2:] * conv_w[2]
    x0, x1, v = jnp.split(u, 3, axis=-1)
    z = (v * x1).astype(F32)
    k = _hyena_filter(L, f_w0, f_b0, f_w1, f_b1, f_w2, f_b2, f_freq, f_out)
    zf = jnp.fft.rfft(z, n=2 * L, axis=1)
    kf = jnp.fft.rfft(k, axis=0)
    y = jnp.fft.irfft(zf * kf[None], n=2 * L, axis=1)[:, :L]
    y = (y + z * d_skip.astype(F32)).astype(h.dtype)
    return (x0 * y) @ w_out + b_out


def _neighbourhood_attention(h, w_qkv, b_qkv, rpb, w_out, b_out):
    B, L, D = h.shape
    rows = L // GRID_W
    kh = min(NA_KH, rows)
    kw = NA_KW
    qkv = h @ w_qkv + b_qkv
    q, k, v = jnp.split(qkv, 3, axis=-1)
    q = q.reshape(B, rows, GRID_W, NA_HEADS, NA_HEAD_DIM) * (NA_HEAD_DIM ** -0.5)
    k = k.reshape(B, rows, GRID_W, NA_HEADS, NA_HEAD_DIM)
    v = v.reshape(B, rows, GRID_W, NA_HEADS, NA_HEAD_DIM)
    cols = jnp.arange(GRID_W)
    col_start = jnp.clip(cols - kw // 2, 0, GRID_W - kw)
    col_idx = col_start[:, None] + jnp.arange(kw)[None, :]
    col_off = col_idx - cols[:, None] + (NA_KW - 1)

    def row_block(r):
        rs = jnp.clip(r - kh // 2, 0, rows - kh)
        q_r = lax.dynamic_index_in_dim(q, r, axis=1, keepdims=False)
        k_slab = lax.dynamic_slice_in_dim(k, rs, kh, axis=1)
        v_slab = lax.dynamic_slice_in_dim(v, rs, kh, axis=1)
        k_win = k_slab[:, :, col_idx]
        v_win = v_slab[:, :, col_idx]
        s = jnp.einsum('bchd,bicjhd->bhcij', q_r, k_win).astype(F32)
        row_off = rs + jnp.arange(kh) - r + (NA_KH - 1)
        bias = rpb[:, row_off][:, :, col_off]
        s = s + jnp.transpose(bias, (0, 2, 1, 3)).astype(F32)[None]
        p = jax.nn.softmax(s.reshape(B, NA_HEADS, GRID_W, kh * kw), axis=-1)
        p = p.reshape(B, NA_HEADS, GRID_W, kh, kw).astype(v.dtype)
        return jnp.einsum('bhcij,bicjhd->bchd', p, v_win)

    out = lax.map(row_block, jnp.arange(rows))
    out = jnp.transpose(out, (1, 0, 2, 3, 4)).reshape(B, L, D)
    return out @ w_out + b_out


def _trunk(x, c, p):
    B = x.shape[0]
    c_act = jax.nn.silu(c)
    for i in range(DEPTH):
        mod = (c_act @ p['ada_w'][i] + p['ada_b'][i]).reshape(B, N_MOD, 1, D_MODEL)
        h = _ada_pre(x, p['norm_g'][i, 0], mod, 0)
        x = x + 0.5 * mod[:, 2] * _swiglu(h, p['ffn_w_gate'][i, 0], p['ffn_w_up'][i, 0], p['ffn_w_down'][i, 0])
        h = _ada_pre(x, p['norm_g'][i, 1], mod, 1)
        j = i // 2
        if i % 2 == 0:
            m = _hyena(h, p['hy_w_in'][j], p['hy_b_in'][j], p['hy_conv_w'][j],
                       p['hy_f_w0'][j], p['hy_f_b0'][j], p['hy_f_w1'][j], p['hy_f_b1'][j],
                       p['hy_f_w2'][j], p['hy_f_b2'][j], p['hy_f_freq'][j], p['hy_f_out'][j],
                       p['hy_d'][j], p['hy_w_out'][j], p['hy_b_out'][j])
        else:
            m = _neighbourhood_attention(h, p['na_w_qkv'][j], p['na_b_qkv'][j], p['na_rpb'][j],
                                         p['na_w_out'][j], p['na_b_out'][j])
        x = x + mod[:, 5] * m
        h = _ada_pre(x, p['norm_g'][i, 2], mod, 2)
        x = x + 0.5 * mod[:, 8] * _swiglu(h, p['ffn_w_gate'][i, 1], p['ffn_w_up'][i, 1], p['ffn_w_down'][i, 1])
    return _rms_norm(x, p['final_g'])


def setup_inputs(seed: int = 0) -> dict:
    key = jax.random.key(seed)
    ks = jax.random.split(key, 32)
    D, F, H = D_MODEL, D_FF, HY_HIDDEN

    def nrm(k, shape, scale):
        return jax.random.normal(k, shape, F32) * scale

    return {
        'x_prompt': nrm(ks[0], (BATCH, SEQ, D), 1.0),
        'x_sample': nrm(ks[1], (DEC_BATCH, DEC_SEQ, D), 1.0),
        'c_prompt': nrm(ks[2], (BATCH, D), 1.0),
        'c_sample': nrm(ks[3], (DEC_BATCH, D), 1.0),
        'norm_g': 1.0 + nrm(ks[4], (DEPTH, N_SUB, D), 0.02),
        'ada_w': nrm(ks[5], (DEPTH, D, N_MOD * D), D ** -0.5),
        'ada_b': nrm(ks[6], (DEPTH, N_MOD * D), 0.02),
        'ffn_w_gate': nrm(ks[7], (DEPTH, 2, D, F), D ** -0.5),
        'ffn_w_up': nrm(ks[8], (DEPTH, 2, D, F), D ** -0.5),
        'ffn_w_down': nrm(ks[9], (DEPTH, 2, F, D), F ** -0.5),
        'hy_w_in': nrm(ks[10], (N_HYENA, D, 3 * D), D ** -0.5),
        'hy_b_in': nrm(ks[11], (N_HYENA, 3 * D), 0.02),
        'hy_conv_w': nrm(ks[12], (N_HYENA, HY_CONV, 3 * D), HY_CONV ** -0.5),
        'hy_f_w0': nrm(ks[13], (N_HYENA, HY_EMB, H), HY_EMB ** -0.5),
        'hy_f_b0': nrm(ks[14], (N_HYENA, H), 0.02),
        'hy_f_w1': nrm(ks[15], (N_HYENA, H, H), H ** -0.5),
        'hy_f_b1': nrm(ks[16], (N_HYENA, H), 0.02),
        'hy_f_w2': nrm(ks[17], (N_HYENA, H, H), H ** -0.5),
        'hy_f_b2': nrm(ks[18], (N_HYENA, H), 0.02),
        'hy_f_freq': 1.0 + nrm(ks[19], (N_HYENA, H), 0.02),
        'hy_f_out': nrm(ks[20], (N_HYENA, H, 2 * D), (H ** -0.5) * HY_FILTER_SCALE),
        'hy_d': nrm(ks[21], (N_HYENA, D), 0.5),
        'hy_w_out': nrm(ks[22], (N_HYENA, D, D), D ** -0.5),
        'hy_b_out': nrm(ks[23], (N_HYENA, D), 0.02),
        'na_w_qkv': nrm(ks[24], (N_NA, D, 3 * D), D ** -0.5),
        'na_b_qkv': nrm(ks[25], (N_NA, 3 * D), 0.02),
        'na_rpb': nrm(ks[26], (N_NA, NA_HEADS, 2 * NA_KH - 1, 2 * NA_KW - 1), 0.1),
        'na_w_out': nrm(ks[27], (N_NA, D, D), D ** -0.5),
        'na_b_out': nrm(ks[28], (N_NA, D), 0.02),
        'final_g': 1.0 + nrm(ks[29], (D,), 0.02),
    }


def reference(x_prompt, x_sample, c_prompt, c_sample, norm_g, ada_w, ada_b,
              ffn_w_gate, ffn_w_up, ffn_w_down,
              hy_w_in, hy_b_in, hy_conv_w, hy_f_w0, hy_f_b0, hy_f_w1, hy_f_b1,
              hy_f_w2, hy_f_b2, hy_f_freq, hy_f_out, hy_d, hy_w_out, hy_b_out,
              na_w_qkv, na_b_qkv, na_rpb, na_w_out, na_b_out, final_g):
    params = dict(norm_g=norm_g, ada_w=ada_w, ada_b=ada_b,
                  ffn_w_gate=ffn_w_gate, ffn_w_up=ffn_w_up, ffn_w_down=ffn_w_down,
                  hy_w_in=hy_w_in, hy_b_in=hy_b_in, hy_conv_w=hy_conv_w,
                  hy_f_w0=hy_f_w0, hy_f_b0=hy_f_b0, hy_f_w1=hy_f_w1, hy_f_b1=hy_f_b1,
                  hy_f_w2=hy_f_w2, hy_f_b2=hy_f_b2, hy_f_freq=hy_f_freq, hy_f_out=hy_f_out,
                  hy_d=hy_d, hy_w_out=hy_w_out, hy_b_out=hy_b_out,
                  na_w_qkv=na_w_qkv, na_b_qkv=na_b_qkv, na_rpb=na_rpb,
                  na_w_out=na_w_out, na_b_out=na_b_out, final_g=final_g)
    y_prompt = _trunk(x_prompt, c_prompt, params)
    y_sample = _trunk(x_sample, c_sample, params)
    return (y_prompt, y_sample)
```

```python
import functools
import math

import jax
import jax.numpy as jnp
from jax import lax
from jax.experimental import pallas as pl
from jax.experimental.pallas import tpu as pltpu

F32 = jnp.float32
BF16 = jnp.bfloat16
HIGHEST = lax.Precision.HIGHEST

EPS = 1e-6
GRID_W = 64
HY_TARGET = 1e-2
HY_MAX_DECAY = math.log(HY_TARGET) / 0.3
HY_MIN_DECAY = math.log(HY_TARGET) / 1.5
MASK_VALUE = -1e30

LANE = 128
SUBLANE = 8
VMEM_LIMIT = 56 * 1024 * 1024


def _params(*sem):
    return pltpu.CompilerParams(dimension_semantics=sem, vmem_limit_bytes=VMEM_LIMIT)


def _tile(n, want, unit):
    if n <= want:
        return n
    t = (want // unit) * unit
    while t > unit and n % t:
        t -= unit
    assert n % t == 0, (n, want, unit)
    return t


def _ada_kernel(cb_ref, w_ref, b_ref, o_ref):
    n_seq = cb_ref.shape[0]
    tn = w_ref.shape[1]
    for t in range(tn // LANE):
        cols = slice(t * LANE, (t + 1) * LANE)
        for s in range(n_seq):
            acc = jnp.sum(w_ref[:, cols] * cb_ref[s], axis=0, keepdims=True)
            o_ref[s:s + 1, cols] = acc + b_ref[:, cols]


def _ada_mod(c_act, ada_w, ada_b):
    depth, k, n = ada_w.shape
    n_seq = c_act.shape[0]
    tn = _tile(n, 512, LANE)
    cb = jnp.broadcast_to(c_act[:, :, None], (n_seq, k, LANE))
    return pl.pallas_call(
        _ada_kernel,
        grid=(depth, n // tn),
        in_specs=[
            pl.BlockSpec((n_seq, k, LANE), lambda l, j: (0, 0, 0)),
            pl.BlockSpec((None, k, tn), lambda l, j: (l, 0, j)),
            pl.BlockSpec((None, 1, tn), lambda l, j: (l, 0, j)),
        ],
        out_specs=pl.BlockSpec((None, n_seq, tn), lambda l, j: (l, 0, j)),
        out_shape=jax.ShapeDtypeStruct((depth, n_seq, n), F32),
        compiler_params=_params("parallel", "parallel"),
        name="ada_mod",
    )(cb, ada_w, ada_b.reshape(depth, 1, n))


def _norm_mod_kernel(x_ref, g_ref, scale_ref, shift_ref, o_ref):
    x = x_ref[...]
    y = x * lax.rsqrt(jnp.mean(x * x, axis=-1, keepdims=True) + EPS)
    h = (y * g_ref[...]) * (1.0 + scale_ref[...]) + shift_ref[...]
    o_ref[...] = h.astype(o_ref.dtype)


def _norm_kernel(x_ref, g_ref, o_ref):
    x = x_ref[...]
    y = x * lax.rsqrt(jnp.mean(x * x, axis=-1, keepdims=True) + EPS)
    o_ref[...] = (y * g_ref[...]).astype(o_ref.dtype)


def _norm_mod(x, g, scale, shift, seq_len):
    t, d = x.shape
    bm = _tile(seq_len, 512, SUBLANE)
    per_seq = seq_len // bm
    vec = pl.BlockSpec((None, 1, d), lambda i: (i // per_seq, 0, 0))
    return pl.pallas_call(
        _norm_mod_kernel,
        grid=(t // bm,),
        in_specs=[
            pl.BlockSpec((bm, d), lambda i: (i, 0)),
            pl.BlockSpec((1, d), lambda i: (0, 0)),
            vec, vec,
        ],
        out_specs=pl.BlockSpec((bm, d), lambda i: (i, 0)),
        out_shape=jax.ShapeDtypeStruct((t, d), BF16),
        compiler_params=_params("parallel"),
        name="norm_mod",
    )(x, g.reshape(1, d), scale[:, None, :], shift[:, None, :])


def _final_norm(x, g, row0, rows):
    _, d = x.shape
    bm = _tile(math.gcd(rows, row0) if row0 else rows, 512, SUBLANE)
    off = row0 // bm
    return pl.pallas_call(
        _norm_kernel,
        grid=(rows // bm,),
        in_specs=[
            pl.BlockSpec((bm, d), lambda i: (i + off, 0)),
            pl.BlockSpec((1, d), lambda i: (0, 0)),
        ],
        out_specs=pl.BlockSpec((bm, d), lambda i: (i, 0)),
        out_shape=jax.ShapeDtypeStruct((rows, d), F32),
        compiler_params=_params("parallel"),
        name="final_norm",
    )(x, g.reshape(1, d))


def _swiglu_kernel(a_ref, wg_ref, wu_ref, o_ref):
    a = a_ref[...]
    g = jnp.dot(a, wg_ref[...], preferred_element_type=F32)
    u = jnp.dot(a, wu_ref[...], preferred_element_type=F32)
    o_ref[...] = (g * jax.nn.sigmoid(g) * u).astype(o_ref.dtype)


def _swiglu_up(a, wg, wu):
    m, k = a.shape
    n = wg.shape[1]
    bm = _tile(m, 1024, SUBLANE)
    bn = _tile(n, 512, LANE)
    return pl.pallas_call(
        _swiglu_kernel,
        grid=(m // bm, n // bn),
        in_specs=[
            pl.BlockSpec((bm, k), lambda i, j: (i, 0)),
            pl.BlockSpec((k, bn), lambda i, j: (0, j)),
            pl.BlockSpec((k, bn), lambda i, j: (0, j)),
        ],
        out_specs=pl.BlockSpec((bm, bn), lambda i, j: (i, j)),
        out_shape=jax.ShapeDtypeStruct((m, n), BF16),
        compiler_params=_params("parallel", "arbitrary"),
        name="swiglu_up",
    )(a, wg, wu)


def _bias_kernel(a_ref, w_ref, b_ref, s_ref, o_ref):
    acc = jnp.dot(a_ref[...], w_ref[...], preferred_element_type=F32)
    o_ref[...] = ((acc + b_ref[...]) * s_ref[...]).astype(o_ref.dtype)


def _proj_bias(a, w, b, col_scale, out_dtype):
    m, k = a.shape
    n = w.shape[1]
    bm = _tile(m, 1024, SUBLANE)
    bn = _tile(n, 1024, LANE)
    return pl.pallas_call(
        _bias_kernel,
        grid=(m // bm, n // bn),
        in_specs=[
            pl.BlockSpec((bm, k), lambda i, j: (i, 0)),
            pl.BlockSpec((k, bn), lambda i, j: (0, j)),
            pl.BlockSpec((1, bn), lambda i, j: (0, j)),
            pl.BlockSpec((1, bn), lambda i, j: (0, j)),
        ],
        out_specs=pl.BlockSpec((bm, bn), lambda i, j: (i, j)),
        out_shape=jax.ShapeDtypeStruct((m, n), out_dtype),
        compiler_params=_params("parallel", "arbitrary"),
        name="proj_bias",
    )(a, w, b.reshape(1, n), col_scale.reshape(1, n))


def _residual_kernel(a_ref, w_ref, b_ref, g_ref, x_ref, o_ref, *scratch, nk):
    part = jnp.dot(a_ref[...], w_ref[...], preferred_element_type=F32)
    if nk == 1:
        o_ref[...] = x_ref[...] + g_ref[...] * (part + b_ref[...])
        return
    acc_ref, = scratch
    kk = pl.program_id(2)

    @pl.when(kk == 0)
    def _():
        acc_ref[...] = part

    @pl.when(kk > 0)
    def _():
        acc_ref[...] += part

    @pl.when(kk == nk - 1)
    def _():
        o_ref[...] = x_ref[...] + g_ref[...] * (acc_ref[...] + b_ref[...])


def _proj_residual(a, w, b, gate, x, seq_len):
    m, k = a.shape
    n = w.shape[1]
    bm = _tile(seq_len, 1024, SUBLANE)
    bk = _tile(k, 4096, LANE)
    nk = k // bk
    bn = _tile(n, 1024 if nk > 1 else 512, LANE)
    per_seq = seq_len // bm
    return pl.pallas_call(
        functools.partial(_residual_kernel, nk=nk),
        grid=(m // bm, n // bn, nk),
        in_specs=[
            pl.BlockSpec((bm, bk), lambda i, j, kk: (i, kk)),
            pl.BlockSpec((bk, bn), lambda i, j, kk: (kk, j)),
            pl.BlockSpec((1, bn), lambda i, j, kk: (0, j)),
            pl.BlockSpec((None, 1, bn), lambda i, j, kk: (i // per_seq, 0, j)),
            pl.BlockSpec((bm, bn), lambda i, j, kk: (i, j)),
        ],
        out_specs=pl.BlockSpec((bm, bn), lambda i, j, kk: (i, j)),
        out_shape=jax.ShapeDtypeStruct((m, n), F32),
        scratch_shapes=[pltpu.VMEM((bm, bn), F32)] if nk > 1 else [],
        input_output_aliases={4: 0},
        compiler_params=_params("parallel", "parallel", "arbitrary"),
        name="proj_residual",
    )(a, w, b.reshape(1, n), gate[:, None, :], x)


def _hy_pre_kernel(u0_ref, u1_ref, uv_ref, c0_ref, c1_ref, cv_ref, x0_ref, z_ref):
    rows = u0_ref.shape[0]
    ridx = lax.broadcasted_iota(jnp.int32, u0_ref.shape, 0)
    first = ridx == 0
    last = ridx == rows - 1

    def conv3(u_ref, c_ref):
        u = u_ref[...]
        prev = jnp.where(first, 0.0, pltpu.roll(u, 1, 0))
        nxt = jnp.where(last, 0.0, pltpu.roll(u, rows - 1, 0))
        return prev * c_ref[0:1, :] + u * c_ref[1:2, :] + nxt * c_ref[2:3, :]

    x0_ref[...] = conv3(u0_ref, c0_ref)
    z_ref[...] = conv3(uv_ref, cv_ref) * conv3(u1_ref, c1_ref)


def _hy_pre(u, conv_w, n_seq, seq_len):
    t, d3 = u.shape
    d = d3 // 3
    dt = LANE
    nd = d // dt
    ublk = lambda part: pl.BlockSpec((seq_len, dt), lambda s, j: (s, part * nd + j))
    cblk = lambda part: pl.BlockSpec((3, dt), lambda s, j: (0, part * nd + j))
    oblk = pl.BlockSpec((seq_len, dt), lambda s, j: (s, j))
    return pl.pallas_call(
        _hy_pre_kernel,
        grid=(n_seq, nd),
        in_specs=[ublk(0), ublk(1), ublk(2), cblk(0), cblk(1), cblk(2)],
        out_specs=[oblk, oblk],
        out_shape=[jax.ShapeDtypeStruct((t, d), F32), jax.ShapeDtypeStruct((t, d), F32)],
        compiler_params=_params("parallel", "parallel"),
        name="hy_pre",
    )(u, u, u, conv_w, conv_w, conv_w)


def _hy_filter_kernel(w0_ref, b0_ref, w1_ref, b1_ref, w2_ref, b2_ref, fr_ref, wo_ref, k_ref,
                      *, seq_len, bands, d_model):
    rows = k_ref.shape[0]
    n = 2 * seq_len
    r = pl.program_id(0) * rows + lax.broadcasted_iota(jnp.int32, (rows, 1), 0)
    m = jnp.where(r < seq_len, r, n - r).astype(F32)
    t = m * (1.0 / (seq_len - 1.0))
    w = (2.0 * math.pi / seq_len) * m
    lane = lax.broadcasted_iota(jnp.int32, (1, LANE), 1)
    band = jnp.where(lane <= bands, lane - 1, lane - 1 - bands).astype(F32)
    f = 1e-4 + band * ((bands - 1 - 1e-4) / (bands - 1))
    ang = w * f
    feat = jnp.where(lane == 0, t,
                     jnp.where(lane <= bands, jnp.cos(ang),
                               jnp.where(lane <= 2 * bands, -jnp.sin(ang), 0.0)))
    fr = fr_ref[...]
    h = jnp.sin(fr * (jnp.dot(feat, w0_ref[...], precision=HIGHEST) + b0_ref[...]))
    h = jnp.sin(fr * (jnp.dot(h, w1_ref[...], precision=HIGHEST) + b1_ref[...]))
    h = jnp.sin(fr * (jnp.dot(h, w2_ref[...], precision=HIGHEST) + b2_ref[...]))
    taps = jnp.dot(h, wo_ref[...], precision=HIGHEST)
    ch = lax.broadcasted_iota(jnp.int32, (1, d_model), 1).astype(F32)
    deltas = jnp.abs(HY_MIN_DECAY + ch * ((HY_MAX_DECAY - HY_MIN_DECAY) / (d_model - 1)))
    k = taps * jnp.exp(-t * deltas)
    k_ref[...] = jnp.where(r == seq_len, 0.0, k)


def _hy_filter(seq_len, f_w0, f_b0, f_w1, f_b1, f_w2, f_b2, f_freq, f_out):
    emb, hid = f_w0.shape
    d = f_out.shape[1] // 2
    bands = (emb - 1) // 2
    assert emb <= LANE and hid <= LANE

    def pad2(a, r, c):
        return jnp.pad(a, ((0, r - a.shape[0]), (0, c - a.shape[1])))

    vec = lambda a: pad2(a.reshape(1, -1), 1, LANE)
    rows = _tile(seq_len, 512, SUBLANE)
    nfwd = seq_len // rows
    full = lambda shape: pl.BlockSpec(shape, lambda i: (0, 0))
    return pl.pallas_call(
        functools.partial(_hy_filter_kernel, seq_len=seq_len, bands=bands, d_model=d),
        grid=(2 * nfwd,),
        in_specs=[
            full((LANE, LANE)), full((1, LANE)),
            full((LANE, LANE)), full((1, LANE)),
            full((LANE, LANE)), full((1, LANE)),
            full((1, LANE)),
            pl.BlockSpec((LANE, d), lambda i: (0, i // nfwd)),
        ],
        out_specs=pl.BlockSpec((rows, d), lambda i: (i, 0)),
        out_shape=jax.ShapeDtypeStruct((2 * seq_len, d), F32),
        compiler_params=_params("parallel"),
        name="hy_filter",
    )(pad2(f_w0, LANE, LANE), vec(f_b0), pad2(f_w1, LANE, LANE), vec(f_b1),
      pad2(f_w2, LANE, LANE), vec(f_b2), vec(f_freq), pad2(f_out, LANE, 2 * d))


def _fft_split(n):
    lg = n.bit_length() - 1
    assert n == 1 << lg
    n1 = 1 << (lg // 2)
    return n1, n // n1


def _dft_tables(n):
    n1, n2 = _fft_split(n)

    def cs(phase, period):
        ang = (2.0 * math.pi / period) * (phase % period).astype(F32)
        return jnp.cos(ang), jnp.sin(ang)

    i1 = jnp.arange(n1, dtype=jnp.int32)
    c1, s1 = cs(i1[:, None] * i1[None, :], n1)
    stage1 = jnp.concatenate([c1, -s1], axis=0)
    stage3 = jnp.concatenate([c1, -s1], axis=1) * (1.0 / n)
    i2 = jnp.arange(n2, dtype=jnp.int32)
    freq = i1[:, None, None] + n1 * i2[None, :, None]
    gc, gs = cs(freq * i2[None, None, :], n)
    fwd = jnp.concatenate([jnp.concatenate([gc, gs], axis=2),
                           jnp.concatenate([-gs, gc], axis=2)], axis=1)
    inv = jnp.swapaxes(fwd, 1, 2)
    return stage1, fwd, inv, stage3


def _dft1_kernel(f_ref, z_ref, o_ref):
    o_ref[...] = jnp.dot(f_ref[...], z_ref[...], precision=HIGHEST, preferred_element_type=F32)


def _dft_stage1(f, zr, lane_tile):
    nb, r, lanes = zr.shape
    rows = f.shape[0]
    return pl.pallas_call(
        _dft1_kernel,
        grid=(nb, lanes // lane_tile),
        in_specs=[
            pl.BlockSpec((rows, r), lambda b, j: (0, 0)),
            pl.BlockSpec((None, r, lane_tile), lambda b, j: (b, 0, j)),
        ],
        out_specs=pl.BlockSpec((None, rows, lane_tile), lambda b, j: (b, 0, j)),
        out_shape=jax.ShapeDtypeStruct((nb, rows, lanes), F32),
        compiler_params=_params("parallel", "parallel"),
        name="dft_stage1",
    )(f, zr)


def _dft2_filter_kernel(mf_ref, a_ref, o_ref):
    n2 = a_ref.shape[1]
    a = a_ref[...].reshape(2 * n2, a_ref.shape[2])
    x = jnp.dot(mf_ref[...], a, precision=HIGHEST, preferred_element_type=F32)
    o_ref[...] = x.reshape(o_ref.shape)


def _dft_stage2_filter(fwd, a5, dt):
    _, n1, n2, d = a5.shape
    return pl.pallas_call(
        _dft2_filter_kernel,
        grid=(n1, d // dt),
        in_specs=[
            pl.BlockSpec((None, 2 * n2, 2 * n2), lambda k, j: (k, 0, 0)),
            pl.BlockSpec((2, None, n2, dt), lambda k, j: (0, k, 0, j)),
        ],
        out_specs=pl.BlockSpec((2, None, n2, dt), lambda k, j: (0, k, 0, j)),
        out_shape=jax.ShapeDtypeStruct(a5.shape, F32),
        compiler_params=_params("parallel", "parallel"),
        name="dft_stage2_filter",
    )(fwd, a5)


def _dft2_kernel(mf_ref, mi_ref, kf_ref, a_ref, o_ref):
    n2 = a_ref.shape[1]
    a = a_ref[...].reshape(2 * n2, a_ref.shape[2])
    x = jnp.dot(mf_ref[...], a, precision=HIGHEST, preferred_element_type=F32)
    xr, xi = x[:n2], x[n2:]
    kr, ki = kf_ref[0], kf_ref[1]
    y = jnp.concatenate([xr * kr - xi * ki, xr * ki + xi * kr], axis=0)
    b = jnp.dot(mi_ref[...], y, precision=HIGHEST, preferred_element_type=F32)
    o_ref[...] = b.reshape(o_ref.shape)


def _dft_stage2(fwd, inv, kf, a5, dt):
    nb, _, n1, n2, d = a5.shape
    ablk = pl.BlockSpec((None, 2, None, n2, dt), lambda k, j, b: (b, 0, k, 0, j))
    mblk = pl.BlockSpec((None, 2 * n2, 2 * n2), lambda k, j, b: (k, 0, 0))
    return pl.pallas_call(
        _dft2_kernel,
        grid=(n1, d // dt, nb),
        in_specs=[
            mblk, mblk,
            pl.BlockSpec((2, None, n2, dt), lambda k, j, b: (0, k, 0, j)),
            ablk,
        ],
        out_specs=ablk,
        out_shape=jax.ShapeDtypeStruct(a5.shape, F32),
        compiler_params=_params("parallel", "parallel", "parallel"),
        name="dft_stage2",
    )(fwd, inv, kf, a5)


def _dft3_kernel(f_ref, b_ref, z_ref, x0_ref, d_ref, o_ref):
    y = jnp.dot(f_ref[...], b_ref[...], precision=HIGHEST, preferred_element_type=F32)
    y = y + z_ref[...] * d_ref[...]
    o_ref[...] = (x0_ref[...] * y).astype(o_ref.dtype)


def _dft_stage3(f, bs, zr, x0r, d_skip):
    nb, r, lanes = zr.shape
    d = d_skip.shape[0]
    rows2 = bs.shape[1]
    vblk = pl.BlockSpec((None, r, d), lambda b, j: (b, 0, j))
    return pl.pallas_call(
        _dft3_kernel,
        grid=(nb, lanes // d),
        in_specs=[
            pl.BlockSpec((r, rows2), lambda b, j: (0, 0)),
            pl.BlockSpec((None, rows2, d), lambda b, j: (b, 0, j)),
            vblk, vblk,
            pl.BlockSpec((1, d), lambda b, j: (0, 0)),
        ],
        out_specs=vblk,
        out_shape=jax.ShapeDtypeStruct((nb, r, lanes), BF16),
        compiler_params=_params("parallel", "parallel"),
        name="dft_stage3",
    )(f, bs, zr, x0r, d_skip.reshape(1, d))


def _hyena_mix(hn, p, j, n_seq, seq_len):
    d = hn.shape[1]
    n = 2 * seq_len
    n1, n2 = _fft_split(n)
    u = _proj_bias(hn, p['hy_w_in'][j], p['hy_b_in'][j], jnp.ones((3 * d,), F32), F32)
    x0, z = _hy_pre(u, p['hy_conv_w'][j], n_seq, seq_len)
    stage1, fwd, inv, stage3 = _dft_tables(n)
    lane_tile = _tile(n2 * d, 4096, LANE)
    dt = _tile(d, 2048, LANE)
    kfil = _hy_filter(seq_len, p['hy_f_w0'][j], p['hy_f_b0'][j], p['hy_f_w1'][j], p['hy_f_b1'][j],
                      p['hy_f_w2'][j], p['hy_f_b2'][j], p['hy_f_freq'][j], p['hy_f_out'][j])
    ka = _dft_stage1(stage1, kfil.reshape(1, n1, n2 * d), lane_tile)
    kf = _dft_stage2_filter(fwd, ka.reshape(2, n1, n2, d), dt)
    half = n1 // 2
    zr = z.reshape(n_seq, half, n2 * d)
    a = _dft_stage1(stage1[:, :half], zr, lane_tile)
    b = _dft_stage2(fwd, inv, kf, a.reshape(n_seq, 2, n1, n2, d), dt)
    out = _dft_stage3(stage3[:half], b.reshape(n_seq, 2 * n1, n2 * d), zr,
                      x0.reshape(n_seq, half, n2 * d), p['hy_d'][j])
    return out.reshape(n_seq * seq_len, d)


def _na_bias_table(rpb, rows):
    heads, nrh, nrw = rpb.shape
    kh_full, kw = (nrh + 1) // 2, (nrw + 1) // 2
    kh = min(kh_full, rows)
    cols = jnp.arange(GRID_W)
    col_start = jnp.clip(cols - kw // 2, 0, GRID_W - kw)
    kc = jnp.arange(GRID_W)
    valid = (kc[None, :] >= col_start[:, None]) & (kc[None, :] < col_start[:, None] + kw)
    col_off = jnp.clip(kc[None, :] - cols[:, None] + (kw - 1), 0, nrw - 1)
    case = jnp.arange(kh)
    i = jnp.arange(kh)
    row_off = jnp.clip(i[None, :] - case[:, None] + (kh_full - 1), 0, nrh - 1)
    tab = rpb[:, row_off[:, :, None, None], col_off[None, None, :, :]]
    tab = jnp.where(valid[None, None, None], tab, MASK_VALUE)
    tab = jnp.transpose(tab, (0, 1, 3, 2, 4))
    return tab.reshape(heads, kh, GRID_W, kh * GRID_W).astype(F32), kh


def _na_kernel(q_ref, k_ref, v_ref, bias_ref, o_ref, *, rows, kh):
    half = kh // 2

    def row(r, carry):
        rs = jnp.clip(r - half, 0, rows - kh)
        q0 = pl.multiple_of(r * GRID_W, GRID_W)
        k0 = pl.multiple_of(rs * GRID_W, GRID_W)
        q = q_ref[pl.ds(q0, GRID_W), :]
        ks = k_ref[pl.ds(k0, kh * GRID_W), :]
        vs = v_ref[pl.ds(k0, kh * GRID_W), :]
        s = lax.dot_general(q, ks, (((1,), (1,)), ((), ())), preferred_element_type=F32)
        s = s + bias_ref[r - rs]
        s = s - jnp.max(s, axis=-1, keepdims=True)
        e = jnp.exp(s)
        prob = (e / jnp.sum(e, axis=-1, keepdims=True)).astype(vs.dtype)
        o = jnp.dot(prob, vs, preferred_element_type=F32)
        o_ref[pl.ds(q0, GRID_W), :] = o.astype(o_ref.dtype)
        return carry

    lax.fori_loop(0, rows, row, 0)


def _na_attention(qkv, rpb, n_seq, seq_len):
    t, d3 = qkv.shape
    d = d3 // 3
    heads = rpb.shape[0]
    hd = d // heads
    assert hd % LANE == 0 and seq_len % GRID_W == 0
    rows = seq_len // GRID_W
    bias, kh = _na_bias_table(rpb, rows)
    blk = lambda part: pl.BlockSpec((seq_len, hd), lambda s, h: (s, part * heads + h))
    return pl.pallas_call(
        functools.partial(_na_kernel, rows=rows, kh=kh),
        grid=(n_seq, heads),
        in_specs=[blk(0), blk(1), blk(2),
                  pl.BlockSpec((None, kh, GRID_W, kh * GRID_W), lambda s, h: (h, 0, 0, 0))],
        out_specs=pl.BlockSpec((seq_len, hd), lambda s, h: (s, h)),
        out_shape=jax.ShapeDtypeStruct((t, d), BF16),
        compiler_params=_params("parallel", "parallel"),
        name="na_attention",
    )(qkv, qkv, qkv, bias)


def _na_mix(hn, p, j, n_seq, seq_len):
    d = hn.shape[1]
    heads = p['na_rpb'].shape[1]
    q_scale = (d // heads) ** -0.5
    col_scale = jnp.concatenate([jnp.full((d,), q_scale, F32), jnp.ones((2 * d,), F32)])
    qkv = _proj_bias(hn, p['na_w_qkv'][j], p['na_b_qkv'][j], col_scale, BF16)
    return _na_attention(qkv, p['na_rpb'][j], n_seq, seq_len)


FFN_PAD = 512


def _ffn(x, hn_args, wg, wu, wd, gate, seq_len):
    hn = _norm_mod(x, *hn_args, seq_len)
    h = _swiglu_up(hn, wg, wu)
    zero_b = jnp.zeros((x.shape[1],), F32)
    return _proj_residual(h, wd, zero_b, 0.5 * gate, x, seq_len)


def kernel(x_prompt, x_sample, c_prompt, c_sample, norm_g, ada_w, ada_b, ffn_w_gate, ffn_w_up, ffn_w_down, hy_w_in, hy_b_in, hy_conv_w, hy_f_w0, hy_f_b0, hy_f_w1, hy_f_b1, hy_f_w2, hy_f_b2, hy_f_freq, hy_f_out, hy_d, hy_w_out, hy_b_out, na_w_qkv, na_b_qkv, na_rpb, na_w_out, na_b_out, final_g):
    nb_p, seq_len, d = x_prompt.shape
    nb_s = x_sample.shape[0]
    assert x_sample.shape[1:] == (seq_len, d)
    n_seq = nb_p + nb_s
    depth = ada_w.shape[0]
    d_ff = ffn_w_gate.shape[-1]
    ff_pad = -d_ff % FFN_PAD

    x = jnp.concatenate([x_prompt.reshape(nb_p * seq_len, d), x_sample.reshape(nb_s * seq_len, d)], axis=0)
    c_act = jax.nn.silu(jnp.concatenate([c_prompt, c_sample], axis=0))
    mod = _ada_mod(c_act, ada_w, ada_b).reshape(depth, n_seq, ada_w.shape[-1] // d, d)

    wg_all = jnp.pad(ffn_w_gate.astype(BF16), ((0, 0), (0, 0), (0, 0), (0, ff_pad)))
    wu_all = jnp.pad(ffn_w_up.astype(BF16), ((0, 0), (0, 0), (0, 0), (0, ff_pad)))
    wd_all = jnp.pad(ffn_w_down.astype(BF16), ((0, 0), (0, 0), (0, ff_pad), (0, 0)))
    p = dict(hy_w_in=hy_w_in.astype(BF16), hy_b_in=hy_b_in, hy_conv_w=hy_conv_w,
             hy_f_w0=hy_f_w0, hy_f_b0=hy_f_b0, hy_f_w1=hy_f_w1, hy_f_b1=hy_f_b1,
             hy_f_w2=hy_f_w2, hy_f_b2=hy_f_b2, hy_f_freq=hy_f_freq, hy_f_out=hy_f_out,
             hy_d=hy_d, hy_w_out=hy_w_out.astype(BF16), hy_b_out=hy_b_out,
             na_w_qkv=na_w_qkv.astype(BF16), na_b_qkv=na_b_qkv, na_rpb=na_rpb,
             na_w_out=na_w_out.astype(BF16), na_b_out=na_b_out)

    for i in range(depth):
        m = mod[i]
        x = _ffn(x, (norm_g[i, 0], m[:, 1], m[:, 0]), wg_all[i, 0], wu_all[i, 0], wd_all[i, 0], m[:, 2], seq_len)
        hn = _norm_mod(x, norm_g[i, 1], m[:, 4], m[:, 3], seq_len)
        j = i // 2
        if i % 2 == 0:
            mixed = _hyena_mix(hn, p, j, n_seq, seq_len)
            x = _proj_residual(mixed, p['hy_w_out'][j], p['hy_b_out'][j], m[:, 5], x, seq_len)
        else:
            mixed = _na_mix(hn, p, j, n_seq, seq_len)
            x = _proj_residual(mixed, p['na_w_out'][j], p['na_b_out'][j], m[:, 5], x, seq_len)
        x = _ffn(x, (norm_g[i, 2], m[:, 7], m[:, 6]), wg_all[i, 1], wu_all[i, 1], wd_all[i, 1], m[:, 8], seq_len)

    y_prompt = _final_norm(x, final_g, 0, nb_p * seq_len).reshape(nb_p, seq_len, d)
    y_sample = _final_norm(x, final_g, nb_p * seq_len, nb_s * seq_len).reshape(nb_s, seq_len, d)
    return (y_prompt, y_sample)
```

```python
import functools
import math

import jax
import jax.numpy as jnp
from jax import lax
from jax.experimental import pallas as pl
from jax.experimental.pallas import tpu as pltpu

F32 = jnp.float32
BF16 = jnp.bfloat16
HIGHEST = lax.Precision.HIGHEST

EPS = 1e-6
GRID_W = 64
HY_TARGET = 1e-2
HY_MAX_DECAY = math.log(HY_TARGET) / 0.3
HY_MIN_DECAY = math.log(HY_TARGET) / 1.5
MASK_VALUE = -1e30

LANE = 128
SUBLANE = 8
VMEM_LIMIT = 56 * 1024 * 1024


def _params(*sem):
    return pltpu.CompilerParams(dimension_semantics=sem, vmem_limit_bytes=VMEM_LIMIT)


def _tile(n, want, unit):
    if n <= want:
        return n
    t = (want // unit) * unit
    while t > unit and n % t:
        t -= unit
    assert n % t == 0, (n, want, unit)
    return t


def _ada_kernel(cb_ref, w_ref, b_ref, o_ref):
    n_seq = cb_ref.shape[0]
    tn = w_ref.shape[1]
    for t in range(tn // LANE):
        cols = slice(t * LANE, (t + 1) * LANE)
        for s in range(n_seq):
            acc = jnp.sum(w_ref[:, cols] * cb_ref[s], axis=0, keepdims=True)
            o_ref[s:s + 1, cols] = acc + b_ref[:, cols]


def _ada_mod(c_act, ada_w, ada_b):
    depth, k, n = ada_w.shape
    n_seq = c_act.shape[0]
    tn = _tile(n, 512, LANE)
    cb = jnp.broadcast_to(c_act[:, :, None], (n_seq, k, LANE))
    return pl.pallas_call(
        _ada_kernel,
        grid=(depth, n // tn),
        in_specs=[
            pl.BlockSpec((n_seq, k, LANE), lambda l, j: (0, 0, 0)),
            pl.BlockSpec((None, k, tn), lambda l, j: (l, 0, j)),
            pl.BlockSpec((None, 1, tn), lambda l, j: (l, 0, j)),
        ],
        out_specs=pl.BlockSpec((None, n_seq, tn), lambda l, j: (l, 0, j)),
        out_shape=jax.ShapeDtypeStruct((depth, n_seq, n), F32),
        compiler_params=_params("parallel", "parallel"),
        name="ada_mod",
    )(cb, ada_w, ada_b.reshape(depth, 1, n))


def _norm_mod_kernel(x_ref, g_ref, scale_ref, shift_ref, o_ref):
    x = x_ref[...]
    y = x * lax.rsqrt(jnp.mean(x * x, axis=-1, keepdims=True) + EPS)
    h = (y * g_ref[...]) * (1.0 + scale_ref[...]) + shift_ref[...]
    o_ref[...] = h.astype(o_ref.dtype)


def _norm_kernel(x_ref, g_ref, o_ref):
    x = x_ref[...]
    y = x * lax.rsqrt(jnp.mean(x * x, axis=-1, keepdims=True) + EPS)
    o_ref[...] = (y * g_ref[...]).astype(o_ref.dtype)


def _norm_mod(x, g, scale, shift, seq_len):
    t, d = x.shape
    bm = _tile(seq_len, 512, SUBLANE)
    per_seq = seq_len // bm
    vec = pl.BlockSpec((None, 1, d), lambda i: (i // per_seq, 0, 0))
    return pl.pallas_call(
        _norm_mod_kernel,
        grid=(t // bm,),
        in_specs=[
            pl.BlockSpec((bm, d), lambda i: (i, 0)),
            pl.BlockSpec((1, d), lambda i: (0, 0)),
            vec, vec,
        ],
        out_specs=pl.BlockSpec((bm, d), lambda i: (i, 0)),
        out_shape=jax.ShapeDtypeStruct((t, d), BF16),
        compiler_params=_params("parallel"),
        name="norm_mod",
    )(x, g.reshape(1, d), scale[:, None, :], shift[:, None, :])


def _final_norm(x, g, row0, rows):
    _, d = x.shape
    bm = _tile(math.gcd(rows, row0) if row0 else rows, 512, SUBLANE)
    off = row0 // bm
    return pl.pallas_call(
        _norm_kernel,
        grid=(rows // bm,),
        in_specs=[
            pl.BlockSpec((bm, d), lambda i: (i + off, 0)),
            pl.BlockSpec((1, d), lambda i: (0, 0)),
        ],
        out_specs=pl.BlockSpec((bm, d), lambda i: (i, 0)),
        out_shape=jax.ShapeDtypeStruct((rows, d), F32),
        compiler_params=_params("parallel"),
        name="final_norm",
    )(x, g.reshape(1, d))


def _swiglu_kernel(a_ref, wg_ref, wu_ref, o_ref):
    a = a_ref[...]
    g = jnp.dot(a, wg_ref[...], preferred_element_type=F32)
    u = jnp.dot(a, wu_ref[...], preferred_element_type=F32)
    o_ref[...] = (g * jax.nn.sigmoid(g) * u).astype(o_ref.dtype)


def _swiglu_up(a, wg, wu):
    m, k = a.shape
    n = wg.shape[1]
    bm = _tile(m, 1024, SUBLANE)
    bn = _tile(n, 512, LANE)
    return pl.pallas_call(
        _swiglu_kernel,
        grid=(m // bm, n // bn),
        in_specs=[
            pl.BlockSpec((bm, k), lambda i, j: (i, 0)),
            pl.BlockSpec((k, bn), lambda i, j: (0, j)),
            pl.BlockSpec((k, bn), lambda i, j: (0, j)),
        ],
        out_specs=pl.BlockSpec((bm, bn), lambda i, j: (i, j)),
        out_shape=jax.ShapeDtypeStruct((m, n), BF16),
        compiler_params=_params("parallel", "arbitrary"),
        name="swiglu_up",
    )(a, wg, wu)


def _bias_kernel(a_ref, w_ref, b_ref, s_ref, o_ref):
    acc = jnp.dot(a_ref[...], w_ref[...], preferred_element_type=F32)
    o_ref[...] = ((acc + b_ref[...]) * s_ref[...]).astype(o_ref.dtype)


def _proj_bias(a, w, b, col_scale, out_dtype):
    m, k = a.shape
    n = w.shape[1]
    bm = _tile(m, 1024, SUBLANE)
    bn = _tile(n, 1024, LANE)
    return pl.pallas_call(
        _bias_kernel,
        grid=(m // bm, n // bn),
        in_specs=[
            pl.BlockSpec((bm, k), lambda i, j: (i, 0)),
            pl.BlockSpec((k, bn), lambda i, j: (0, j)),
            pl.BlockSpec((1, bn), lambda i, j: (0, j)),
            pl.BlockSpec((1, bn), lambda i, j: (0, j)),
        ],
        out_specs=pl.BlockSpec((bm, bn), lambda i, j: (i, j)),
        out_shape=jax.ShapeDtypeStruct((m, n), out_dtype),
        compiler_params=_params("parallel", "arbitrary"),
        name="proj_bias",
    )(a, w, b.reshape(1, n), col_scale.reshape(1, n))


def _residual_kernel(a_ref, w_ref, b_ref, g_ref, x_ref, o_ref, *scratch, nk):
    if nk == 1:
        part = jnp.dot(a_ref[...], w_ref[...], preferred_element_type=F32)
        o_ref[...] = x_ref[...] + g_ref[...] * (part + b_ref[...])
        return
    acc_ref, = scratch
    kk = pl.program_id(2)

    @pl.when(kk == 0)
    def _():
        acc_ref[...] = jnp.zeros_like(acc_ref)

    acc_ref[...] += jnp.dot(a_ref[...], w_ref[...], preferred_element_type=F32)

    @pl.when(kk == nk - 1)
    def _():
        o_ref[...] = x_ref[...] + g_ref[...] * (acc_ref[...] + b_ref[...])


def _proj_residual(a, w, b, gate, x, seq_len):
    m, k = a.shape
    n = w.shape[1]
    bm = _tile(seq_len, 1024, SUBLANE)
    bk = _tile(k, 4096, LANE)
    nk = k // bk
    bn = _tile(n, 1024 if nk > 1 else 512, LANE)
    per_seq = seq_len // bm
    return pl.pallas_call(
        functools.partial(_residual_kernel, nk=nk),
        grid=(m // bm, n // bn, nk),
        in_specs=[
            pl.BlockSpec((bm, bk), lambda i, j, kk: (i, kk)),
            pl.BlockSpec((bk, bn), lambda i, j, kk: (kk, j)),
            pl.BlockSpec((1, bn), lambda i, j, kk: (0, j)),
            pl.BlockSpec((None, 1, bn), lambda i, j, kk: (i // per_seq, 0, j)),
            pl.BlockSpec((bm, bn), lambda i, j, kk: (i, j)),
        ],
        out_specs=pl.BlockSpec((bm, bn), lambda i, j, kk: (i, j)),
        out_shape=jax.ShapeDtypeStruct((m, n), F32),
        scratch_shapes=[pltpu.VMEM((bm, bn), F32)] if nk > 1 else [],
        input_output_aliases={4: 0},
        compiler_params=_params("parallel", "parallel", "arbitrary"),
        name="proj_residual",
    )(a, w, b.reshape(1, n), gate[:, None, :], x)


def _hy_pre_kernel(u0_ref, u1_ref, uv_ref, c0_ref, c1_ref, cv_ref, x0_ref, z_ref):
    rows = u0_ref.shape[0]
    ridx = lax.broadcasted_iota(jnp.int32, u0_ref.shape, 0)
    first = ridx == 0
    last = ridx == rows - 1

    def conv3(u_ref, c_ref):
        u = u_ref[...]
        prev = jnp.where(first, 0.0, pltpu.roll(u, 1, 0))
        nxt = jnp.where(last, 0.0, pltpu.roll(u, rows - 1, 0))
        return prev * c_ref[0:1, :] + u * c_ref[1:2, :] + nxt * c_ref[2:3, :]

    x0_ref[...] = conv3(u0_ref, c0_ref)
    z_ref[...] = conv3(uv_ref, cv_ref) * conv3(u1_ref, c1_ref)


def _hy_pre(u, conv_w, n_seq, seq_len):
    t, d3 = u.shape
    d = d3 // 3
    dt = LANE
    nd = d // dt
    ublk = lambda part: pl.BlockSpec((seq_len, dt), lambda s, j: (s, part * nd + j))
    cblk = lambda part: pl.BlockSpec((3, dt), lambda s, j: (0, part * nd + j))
    oblk = pl.BlockSpec((seq_len, dt), lambda s, j: (s, j))
    return pl.pallas_call(
        _hy_pre_kernel,
        grid=(n_seq, nd),
        in_specs=[ublk(0), ublk(1), ublk(2), cblk(0), cblk(1), cblk(2)],
        out_specs=[oblk, oblk],
        out_shape=[jax.ShapeDtypeStruct((t, d), F32), jax.ShapeDtypeStruct((t, d), F32)],
        compiler_params=_params("parallel", "parallel"),
        name="hy_pre",
    )(u, u, u, conv_w, conv_w, conv_w)


def _hy_filter_kernel(w0_ref, b0_ref, w1_ref, b1_ref, w2_ref, b2_ref, fr_ref, wo_ref, k_ref,
                      *, seq_len, bands, d_model):
    rows = k_ref.shape[0]
    n = 2 * seq_len
    r = pl.program_id(0) * rows + lax.broadcasted_iota(jnp.int32, (rows, 1), 0)
    m = jnp.where(r < seq_len, r, n - r).astype(F32)
    t = m * (1.0 / (seq_len - 1.0))
    w = (2.0 * math.pi / seq_len) * m
    lane = lax.broadcasted_iota(jnp.int32, (1, LANE), 1)
    band = jnp.where(lane <= bands, lane - 1, lane - 1 - bands).astype(F32)
    f = 1e-4 + band * ((bands - 1 - 1e-4) / (bands - 1))
    ang = w * f
    feat = jnp.where(lane == 0, t,
                     jnp.where(lane <= bands, jnp.cos(ang),
                               jnp.where(lane <= 2 * bands, -jnp.sin(ang), 0.0)))
    fr = fr_ref[...]
    h = jnp.sin(fr * (jnp.dot(feat, w0_ref[...], precision=HIGHEST) + b0_ref[...]))
    h = jnp.sin(fr * (jnp.dot(h, w1_ref[...], precision=HIGHEST) + b1_ref[...]))
    h = jnp.sin(fr * (jnp.dot(h, w2_ref[...], precision=HIGHEST) + b2_ref[...]))
    taps = jnp.dot(h, wo_ref[...], precision=HIGHEST)
    ch = lax.broadcasted_iota(jnp.int32, (1, d_model), 1).astype(F32)
    deltas = jnp.abs(HY_MIN_DECAY + ch * ((HY_MAX_DECAY - HY_MIN_DECAY) / (d_model - 1)))
    k = taps * jnp.exp(-t * deltas)
    k_ref[...] = jnp.where(r == seq_len, 0.0, k)


def _hy_filter(seq_len, f_w0, f_b0, f_w1, f_b1, f_w2, f_b2, f_freq, f_out):
    emb, hid = f_w0.shape
    d = f_out.shape[1] // 2
    bands = (emb - 1) // 2
    assert emb <= LANE and hid <= LANE

    def pad2(a, r, c):
        return jnp.pad(a, ((0, r - a.shape[0]), (0, c - a.shape[1])))

    vec = lambda a: pad2(a.reshape(1, -1), 1, LANE)
    rows = _tile(seq_len, 512, SUBLANE)
    nfwd = seq_len // rows
    full = lambda shape: pl.BlockSpec(shape, lambda i: (0, 0))
    return pl.pallas_call(
        functools.partial(_hy_filter_kernel, seq_len=seq_len, bands=bands, d_model=d),
        grid=(2 * nfwd,),
        in_specs=[
            full((LANE, LANE)), full((1, LANE)),
            full((LANE, LANE)), full((1, LANE)),
            full((LANE, LANE)), full((1, LANE)),
            full((1, LANE)),
            pl.BlockSpec((LANE, d), lambda i: (0, i // nfwd)),
        ],
        out_specs=pl.BlockSpec((rows, d), lambda i: (i, 0)),
        out_shape=jax.ShapeDtypeStruct((2 * seq_len, d), F32),
        compiler_params=_params("parallel"),
        name="hy_filter",
    )(pad2(f_w0, LANE, LANE), vec(f_b0), pad2(f_w1, LANE, LANE), vec(f_b1),
      pad2(f_w2, LANE, LANE), vec(f_b2), vec(f_freq), pad2(f_out, LANE, 2 * d))


def _fft_split(n):
    lg = n.bit_length() - 1
    assert n == 1 << lg
    n1 = 1 << (lg // 2)
    return n1, n // n1


def _dft_tables(n):
    n1, n2 = _fft_split(n)

    def cs(phase, period):
        ang = (2.0 * math.pi / period) * (phase % period).astype(F32)
        return jnp.cos(ang), jnp.sin(ang)

    i1 = jnp.arange(n1, dtype=jnp.int32)
    c1, s1 = cs(i1[:, None] * i1[None, :], n1)
    stage1 = jnp.concatenate([c1, -s1], axis=0)
    stage3 = jnp.concatenate([c1, -s1], axis=1) * (1.0 / n)
    i2 = jnp.arange(n2, dtype=jnp.int32)
    freq = i1[:, None, None] + n1 * i2[None, :, None]
    gc, gs = cs(freq * i2[None, None, :], n)
    fwd = jnp.concatenate([jnp.concatenate([gc, gs], axis=2),
                           jnp.concatenate([-gs, gc], axis=2)], axis=1)
    inv = jnp.swapaxes(fwd, 1, 2)
    return stage1, fwd, inv, stage3


def _split_hi_lo(x):
    bits = lax.bitcast_convert_type(x, jnp.uint32) & jnp.uint32(0xFFFF0000)
    hi = lax.bitcast_convert_type(bits, F32)
    return hi.astype(BF16), (x - hi).astype(BF16)


def _cat3(m):
    hi, lo = _split_hi_lo(m)
    return jnp.concatenate([hi, hi, lo], axis=-1)


def _stack3(x):
    hi, lo = _split_hi_lo(x)
    return jnp.concatenate([hi, lo, hi], axis=0)


def _dot3(f3, x):
    return jnp.dot(f3, _stack3(x), preferred_element_type=F32)


FFT_ROWS = 16


def _dft1_kernel(f_ref, z_ref, o_ref):
    f = f_ref[...]
    zt = jnp.swapaxes(z_ref[...], 0, 1)
    out = jnp.stack([_dot3(f, zt[s]) for s in range(zt.shape[0])], axis=0)
    o_ref[...] = jnp.swapaxes(out, 0, 1)


def _dft_stage1(f3, z3, nb):
    rows_all, n2, d = z3.shape
    r = rows_all // nb
    m = f3.shape[0]
    st = _tile(n2, FFT_ROWS, FFT_ROWS)
    dt = _tile(d, 512, LANE)
    return pl.pallas_call(
        _dft1_kernel,
        grid=(nb, n2 // st, d // dt),
        in_specs=[
            pl.BlockSpec((m, 3 * r), lambda b, i, j: (0, 0)),
            pl.BlockSpec((r, st, dt), lambda b, i, j: (b, i, j)),
        ],
        out_specs=pl.BlockSpec((None, m, st, dt), lambda b, i, j: (b, 0, i, j)),
        out_shape=jax.ShapeDtypeStruct((nb, m, n2, d), F32),
        compiler_params=_params("parallel", "parallel", "parallel"),
        name="dft_stage1",
    )(f3, z3)


def _dft2_filter_kernel(mf_ref, a_ref, o_ref):
    n2 = a_ref.shape[1]
    a = a_ref[...].reshape(2 * n2, a_ref.shape[2])
    o_ref[...] = _dot3(mf_ref[...], a).reshape(o_ref.shape)


def _dft_stage2_filter(fwd3, a5, dt):
    _, n1, n2, d = a5.shape
    return pl.pallas_call(
        _dft2_filter_kernel,
        grid=(n1, d // dt),
        in_specs=[
            pl.BlockSpec((None, 2 * n2, 6 * n2), lambda k, j: (k, 0, 0)),
            pl.BlockSpec((2, None, n2, dt), lambda k, j: (0, k, 0, j)),
        ],
        out_specs=pl.BlockSpec((2, None, n2, dt), lambda k, j: (0, k, 0, j)),
        out_shape=jax.ShapeDtypeStruct(a5.shape, F32),
        compiler_params=_params("parallel", "parallel"),
        name="dft_stage2_filter",
    )(fwd3, a5)


def _dft2_kernel(mf_ref, mi_ref, kf_ref, a_ref, o_ref):
    n2 = a_ref.shape[1]
    a = a_ref[...].reshape(2 * n2, a_ref.shape[2])
    x = _dot3(mf_ref[...], a)
    xr, xi = x[:n2], x[n2:]
    kr, ki = kf_ref[0], kf_ref[1]
    y = jnp.concatenate([xr * kr - xi * ki, xr * ki + xi * kr], axis=0)
    o_ref[...] = _dot3(mi_ref[...], y).reshape(o_ref.shape)


def _dft_stage2(fwd3, inv3, kf, a5, dt):
    nb, _, n1, n2, d = a5.shape
    ablk = pl.BlockSpec((None, 2, None, n2, dt), lambda k, j, b: (b, 0, k, 0, j))
    mblk = pl.BlockSpec((None, 2 * n2, 6 * n2), lambda k, j, b: (k, 0, 0))
    return pl.pallas_call(
        _dft2_kernel,
        grid=(n1, d // dt, nb),
        in_specs=[
            mblk, mblk,
            pl.BlockSpec((2, None, n2, dt), lambda k, j, b: (0, k, 0, j)),
            ablk,
        ],
        out_specs=ablk,
        out_shape=jax.ShapeDtypeStruct(a5.shape, F32),
        compiler_params=_params("parallel", "parallel", "parallel"),
        name="dft_stage2",
    )(fwd3, inv3, kf, a5)


def _dft3_kernel(f_ref, b_ref, z_ref, x0_ref, d_ref, o_ref):
    f = f_ref[...]
    bt = jnp.swapaxes(b_ref[...], 0, 1)
    y = jnp.stack([_dot3(f, bt[s]) for s in range(bt.shape[0])], axis=0)
    y = jnp.swapaxes(y, 0, 1) + z_ref[...] * d_ref[...]
    o_ref[...] = (x0_ref[...] * y).astype(o_ref.dtype)


def _dft_stage3(f3, b4, z3, x03, d_skip):
    nb, rows2, n2, d = b4.shape
    r = z3.shape[0] // nb
    st = _tile(n2, FFT_ROWS, FFT_ROWS)
    dt = _tile(d, 512, LANE)
    vblk = pl.BlockSpec((r, st, dt), lambda b, i, j: (b, i, j))
    return pl.pallas_call(
        _dft3_kernel,
        grid=(nb, n2 // st, d // dt),
        in_specs=[
            pl.BlockSpec((r, 3 * rows2), lambda b, i, j: (0, 0)),
            pl.BlockSpec((None, rows2, st, dt), lambda b, i, j: (b, 0, i, j)),
            vblk, vblk,
            pl.BlockSpec((1, dt), lambda b, i, j: (0, j)),
        ],
        out_specs=vblk,
        out_shape=jax.ShapeDtypeStruct(z3.shape, BF16),
        compiler_params=_params("parallel", "parallel", "parallel"),
        name="dft_stage3",
    )(f3, b4, z3, x03, d_skip.reshape(1, d))


def _hyena_mix(hn, p, j, n_seq, seq_len, tables):
    d = hn.shape[1]
    n1, n2 = _fft_split(2 * seq_len)
    stage1, stage1_half, fwd, inv, stage3_half = tables
    u = _proj_bias(hn, p['hy_w_in'][j], p['hy_b_in'][j], jnp.ones((3 * d,), F32), F32)
    x0, z = _hy_pre(u, p['hy_conv_w'][j], n_seq, seq_len)
    dt = _tile(d, 2048, LANE)
    kfil = _hy_filter(seq_len, p['hy_f_w0'][j], p['hy_f_b0'][j], p['hy_f_w1'][j], p['hy_f_b1'][j],
                      p['hy_f_w2'][j], p['hy_f_b2'][j], p['hy_f_freq'][j], p['hy_f_out'][j])
    ka = _dft_stage1(stage1, kfil.reshape(n1, n2, d), 1)
    kf = _dft_stage2_filter(fwd, ka.reshape(2, n1, n2, d), dt)
    half = n1 // 2
    z3 = z.reshape(n_seq * half, n2, d)
    a = _dft_stage1(stage1_half, z3, n_seq)
    b = _dft_stage2(fwd, inv, kf, a.reshape(n_seq, 2, n1, n2, d), dt)
    out = _dft_stage3(stage3_half, b.reshape(n_seq, 2 * n1, n2, d), z3,
                      x0.reshape(n_seq * half, n2, d), p['hy_d'][j])
    return out.reshape(n_seq * seq_len, d)


def _hyena_tables(seq_len):
    stage1, fwd, inv, stage3 = _dft_tables(2 * seq_len)
    half = stage1.shape[1] // 2
    return (_cat3(stage1), _cat3(stage1[:, :half]), _cat3(fwd), _cat3(inv), _cat3(stage3[:half]))


def _na_bias_table(rpb, rows):
    heads, nrh, nrw = rpb.shape
    kh_full, kw = (nrh + 1) // 2, (nrw + 1) // 2
    kh = min(kh_full, rows)
    w = GRID_W
    assert kw <= w and nrw <= 2 * w - 1
    cols = jnp.arange(w)
    col_start = jnp.clip(cols - kw // 2, 0, w - kw)
    valid = (cols[None, :] >= col_start[:, None]) & (cols[None, :] < col_start[:, None] + kw)
    u = jnp.pad(rpb, ((0, 0), (0, 0), (w - kw, 2 * w - (w - kw) - nrw)))
    flat = jnp.tile(u, (1, 1, w))[:, :, w - 1:w - 1 + w * (2 * w - 1)]
    toep = flat.reshape(heads, nrh, w, 2 * w - 1)[:, :, :, :w]
    toep = jnp.where(valid[None, None], toep, MASK_VALUE)
    tab = jnp.stack([toep[:, kh_full - 1 - case:kh_full - 1 - case + kh] for case in range(kh)], axis=1)
    tab = jnp.transpose(tab, (0, 1, 3, 2, 4))
    return tab.reshape(heads, kh, w, kh * w).astype(F32), kh


NA_CHUNK = 32
NA_UNROLL = 8


def _na_kernel(q_ref, k_ref, v_ref, bias_ref, o_ref, s_ref, p_ref, *, rows, kh):
    half = kh // 2
    chunk = s_ref.shape[0] // GRID_W
    keys = kh * GRID_W

    def chunk_body(c, carry):
        r0 = c * chunk

        def scores(i, carry):
            r = r0 + i
            rs = jnp.clip(r - half, 0, rows - kh)
            q = q_ref[pl.ds(pl.multiple_of(r * GRID_W, GRID_W), GRID_W), :]
            ks = k_ref[pl.ds(pl.multiple_of(rs * GRID_W, GRID_W), keys), :]
            s = lax.dot_general(q, ks, (((1,), (1,)), ((), ())), preferred_element_type=F32)
            s_ref[pl.ds(pl.multiple_of(i * GRID_W, GRID_W), GRID_W), :] = s + bias_ref[r - rs]
            return carry

        def softmax(i, carry):
            rows_i = pl.ds(pl.multiple_of(i * GRID_W, GRID_W), GRID_W)
            s = s_ref[rows_i, :]
            e = jnp.exp(s - jnp.max(s, axis=-1, keepdims=True))
            inv = 1.0 / jnp.sum(e, axis=-1, keepdims=True)
            p_ref[rows_i, :] = (e * inv).astype(p_ref.dtype)
            return carry

        def values(i, carry):
            r = r0 + i
            rs = jnp.clip(r - half, 0, rows - kh)
            vs = v_ref[pl.ds(pl.multiple_of(rs * GRID_W, GRID_W), keys), :]
            prob = p_ref[pl.ds(pl.multiple_of(i * GRID_W, GRID_W), GRID_W), :]
            o = jnp.dot(prob, vs, preferred_element_type=F32)
            o_ref[pl.ds(pl.multiple_of(r * GRID_W, GRID_W), GRID_W), :] = o.astype(o_ref.dtype)
            return carry

        lax.fori_loop(0, chunk, scores, 0, unroll=NA_UNROLL)
        lax.fori_loop(0, chunk, softmax, 0, unroll=NA_UNROLL)
        lax.fori_loop(0, chunk, values, 0, unroll=NA_UNROLL)
        return carry

    lax.fori_loop(0, rows // chunk, chunk_body, 0)


def _na_attention(qkv, rpb, n_seq, seq_len):
    t, d3 = qkv.shape
    d = d3 // 3
    heads = rpb.shape[0]
    hd = d // heads
    assert hd % LANE == 0 and seq_len % GRID_W == 0
    rows = seq_len // GRID_W
    bias, kh = _na_bias_table(rpb, rows)
    chunk = _tile(rows, NA_CHUNK, NA_UNROLL)
    blk = lambda part: pl.BlockSpec((seq_len, hd), lambda s, h: (s, part * heads + h))
    return pl.pallas_call(
        functools.partial(_na_kernel, rows=rows, kh=kh),
        grid=(n_seq, heads),
        in_specs=[blk(0), blk(1), blk(2),
                  pl.BlockSpec((None, kh, GRID_W, kh * GRID_W), lambda s, h: (h, 0, 0, 0))],
        out_specs=pl.BlockSpec((seq_len, hd), lambda s, h: (s, h)),
        out_shape=jax.ShapeDtypeStruct((t, d), BF16),
        scratch_shapes=[pltpu.VMEM((chunk * GRID_W, kh * GRID_W), F32),
                        pltpu.VMEM((chunk * GRID_W, kh * GRID_W), BF16)],
        compiler_params=_params("parallel", "parallel"),
        name="na_attention",
    )(qkv, qkv, qkv, bias)


def _na_mix(hn, p, j, n_seq, seq_len):
    d = hn.shape[1]
    heads = p['na_rpb'].shape[1]
    q_scale = (d // heads) ** -0.5
    col_scale = jnp.concatenate([jnp.full((d,), q_scale, F32), jnp.ones((2 * d,), F32)])
    qkv = _proj_bias(hn, p['na_w_qkv'][j], p['na_b_qkv'][j], col_scale, BF16)
    return _na_attention(qkv, p['na_rpb'][j], n_seq, seq_len)


FFN_PAD = 512


def _ffn(x, hn_args, wg, wu, wd, gate, seq_len):
    hn = _norm_mod(x, *hn_args, seq_len)
    h = _swiglu_up(hn, wg, wu)
    zero_b = jnp.zeros((x.shape[1],), F32)
    return _proj_residual(h, wd, zero_b, 0.5 * gate, x, seq_len)


def kernel(x_prompt, x_sample, c_prompt, c_sample, norm_g, ada_w, ada_b, ffn_w_gate, ffn_w_up, ffn_w_down, hy_w_in, hy_b_in, hy_conv_w, hy_f_w0, hy_f_b0, hy_f_w1, hy_f_b1, hy_f_w2, hy_f_b2, hy_f_freq, hy_f_out, hy_d, hy_w_out, hy_b_out, na_w_qkv, na_b_qkv, na_rpb, na_w_out, na_b_out, final_g):
    nb_p, seq_len, d = x_prompt.shape
    nb_s = x_sample.shape[0]
    assert x_sample.shape[1:] == (seq_len, d)
    n_seq = nb_p + nb_s
    depth = ada_w.shape[0]
    d_ff = ffn_w_gate.shape[-1]
    ff_pad = -d_ff % FFN_PAD

    x = jnp.concatenate([x_prompt.reshape(nb_p * seq_len, d), x_sample.reshape(nb_s * seq_len, d)], axis=0)
    c_act = jax.nn.silu(jnp.concatenate([c_prompt, c_sample], axis=0))
    mod = _ada_mod(c_act, ada_w, ada_b).reshape(depth, n_seq, ada_w.shape[-1] // d, d)

    wg_all = jnp.pad(ffn_w_gate.astype(BF16), ((0, 0), (0, 0), (0, 0), (0, ff_pad)))
    wu_all = jnp.pad(ffn_w_up.astype(BF16), ((0, 0), (0, 0), (0, 0), (0, ff_pad)))
    wd_all = jnp.pad(ffn_w_down.astype(BF16), ((0, 0), (0, 0), (0, ff_pad), (0, 0)))
    p = dict(hy_w_in=hy_w_in.astype(BF16), hy_b_in=hy_b_in, hy_conv_w=hy_conv_w,
             hy_f_w0=hy_f_w0, hy_f_b0=hy_f_b0, hy_f_w1=hy_f_w1, hy_f_b1=hy_f_b1,
             hy_f_w2=hy_f_w2, hy_f_b2=hy_f_b2, hy_f_freq=hy_f_freq, hy_f_out=hy_f_out,
             hy_d=hy_d, hy_w_out=hy_w_out.astype(BF16), hy_b_out=hy_b_out,
             na_w_qkv=na_w_qkv.astype(BF16), na_b_qkv=na_b_qkv, na_rpb=na_rpb,
             na_w_out=na_w_out.astype(BF16), na_b_out=na_b_out)

    tables = _hyena_tables(seq_len)
    for i in range(depth):
        m = mod[i]
        x = _ffn(x, (norm_g[i, 0], m[:, 1], m[:, 0]), wg_all[i, 0], wu_all[i, 0], wd_all[i, 0], m[:, 2], seq_len)
        hn = _norm_mod(x, norm_g[i, 1], m[:, 4], m[:, 3], seq_len)
        j = i // 2
        if i % 2 == 0:
            mixed = _hyena_mix(hn, p, j, n_seq, seq_len, tables)
            x = _proj_residual(mixed, p['hy_w_out'][j], p['hy_b_out'][j], m[:, 5], x, seq_len)
        else:
            mixed = _na_mix(hn, p, j, n_seq, seq_len)
            x = _proj_residual(mixed, p['na_w_out'][j], p['na_b_out'][j], m[:, 5], x, seq_len)
        x = _ffn(x, (norm_g[i, 2], m[:, 7], m[:, 6]), wg_all[i, 1], wu_all[i, 1], wd_all[i, 1], m[:, 8], seq_len)

    y_prompt = _final_norm(x, final_g, 0, nb_p * seq_len).reshape(nb_p, seq_len, d)
    y_sample = _final_norm(x, final_g, nb_p * seq_len, nb_s * seq_len).reshape(nb_s, seq_len, d)
    return (y_prompt, y_sample)
```

```python
import functools
import math

import jax
import jax.numpy as jnp
from jax import lax
from jax.experimental import pallas as pl
from jax.experimental.pallas import tpu as pltpu

F32 = jnp.float32
BF16 = jnp.bfloat16
HIGHEST = lax.Precision.HIGHEST

EPS = 1e-6
GRID_W = 64
HY_TARGET = 1e-2
HY_MAX_DECAY = math.log(HY_TARGET) / 0.3
HY_MIN_DECAY = math.log(HY_TARGET) / 1.5
MASK_VALUE = -1e30

LANE = 128
SUBLANE = 8
VMEM_LIMIT = 56 * 1024 * 1024


def _params(*sem):
    return pltpu.CompilerParams(dimension_semantics=sem, vmem_limit_bytes=VMEM_LIMIT)


def _tile(n, want, unit):
    if n <= want:
        return n
    t = (want // unit) * unit
    while t > unit and n % t:
        t -= unit
    assert n % t == 0, (n, want, unit)
    return t


def _ada_kernel(cb_ref, w_ref, b_ref, o_ref):
    n_seq = cb_ref.shape[0]
    tn = w_ref.shape[1]
    for t in range(tn // LANE):
        cols = slice(t * LANE, (t + 1) * LANE)
        for s in range(n_seq):
            acc = jnp.sum(w_ref[:, cols] * cb_ref[s], axis=0, keepdims=True)
            o_ref[s:s + 1, cols] = acc + b_ref[:, cols]


def _ada_mod(c_act, ada_w, ada_b):
    depth, k, n = ada_w.shape
    n_seq = c_act.shape[0]
    tn = _tile(n, 512, LANE)
    cb = jnp.broadcast_to(c_act[:, :, None], (n_seq, k, LANE))
    return pl.pallas_call(
        _ada_kernel,
        grid=(depth, n // tn),
        in_specs=[
            pl.BlockSpec((n_seq, k, LANE), lambda l, j: (0, 0, 0)),
            pl.BlockSpec((None, k, tn), lambda l, j: (l, 0, j)),
            pl.BlockSpec((None, 1, tn), lambda l, j: (l, 0, j)),
        ],
        out_specs=pl.BlockSpec((None, n_seq, tn), lambda l, j: (l, 0, j)),
        out_shape=jax.ShapeDtypeStruct((depth, n_seq, n), F32),
        compiler_params=_params("parallel", "parallel"),
        name="ada_mod",
    )(cb, ada_w, ada_b.reshape(depth, 1, n))


def _norm_mod_kernel(x_ref, g_ref, scale_ref, shift_ref, o_ref):
    x = x_ref[...]
    y = x * lax.rsqrt(jnp.mean(x * x, axis=-1, keepdims=True) + EPS)
    h = (y * g_ref[...]) * (1.0 + scale_ref[...]) + shift_ref[...]
    o_ref[...] = h.astype(o_ref.dtype)


def _norm_kernel(x_ref, g_ref, o_ref):
    x = x_ref[...]
    y = x * lax.rsqrt(jnp.mean(x * x, axis=-1, keepdims=True) + EPS)
    o_ref[...] = (y * g_ref[...]).astype(o_ref.dtype)


def _norm_mod(x, g, scale, shift, seq_len):
    t, d = x.shape
    bm = _tile(seq_len, 512, SUBLANE)
    per_seq = seq_len // bm
    vec = pl.BlockSpec((None, 1, d), lambda i: (i // per_seq, 0, 0))
    return pl.pallas_call(
        _norm_mod_kernel,
        grid=(t // bm,),
        in_specs=[
            pl.BlockSpec((bm, d), lambda i: (i, 0)),
            pl.BlockSpec((1, d), lambda i: (0, 0)),
            vec, vec,
        ],
        out_specs=pl.BlockSpec((bm, d), lambda i: (i, 0)),
        out_shape=jax.ShapeDtypeStruct((t, d), BF16),
        compiler_params=_params("parallel"),
        name="norm_mod",
    )(x, g.reshape(1, d), scale[:, None, :], shift[:, None, :])


def _final_norm(x, g, row0, rows):
    _, d = x.shape
    bm = _tile(math.gcd(rows, row0) if row0 else rows, 512, SUBLANE)
    off = row0 // bm
    return pl.pallas_call(
        _norm_kernel,
        grid=(rows // bm,),
        in_specs=[
            pl.BlockSpec((bm, d), lambda i: (i + off, 0)),
            pl.BlockSpec((1, d), lambda i: (0, 0)),
        ],
        out_specs=pl.BlockSpec((bm, d), lambda i: (i, 0)),
        out_shape=jax.ShapeDtypeStruct((rows, d), F32),
        compiler_params=_params("parallel"),
        name="final_norm",
    )(x, g.reshape(1, d))


SWIGLU_TILE_ELEMS = 1024 * 512


def _swiglu_kernel(a_ref, wg_ref, wu_ref, o_ref):
    a = a_ref[...]
    g = jnp.dot(a, wg_ref[...], preferred_element_type=F32)
    u = jnp.dot(a, wu_ref[...], preferred_element_type=F32)
    o_ref[...] = (g * jax.nn.sigmoid(g) * u).astype(o_ref.dtype)


def _swiglu_up(a, wg, wu):
    m, k = a.shape
    n = wg.shape[1]
    bn = next(c for c in (512, 256, LANE) if n % c == 0)
    bm = _tile(m, SWIGLU_TILE_ELEMS // bn, SUBLANE)
    return pl.pallas_call(
        _swiglu_kernel,
        grid=(m // bm, n // bn),
        in_specs=[
            pl.BlockSpec((bm, k), lambda i, j: (i, 0)),
            pl.BlockSpec((k, bn), lambda i, j: (0, j)),
            pl.BlockSpec((k, bn), lambda i, j: (0, j)),
        ],
        out_specs=pl.BlockSpec((bm, bn), lambda i, j: (i, j)),
        out_shape=jax.ShapeDtypeStruct((m, n), BF16),
        compiler_params=_params("parallel", "arbitrary"),
        name="swiglu_up",
    )(a, wg, wu)


def _bias_kernel(a_ref, w_ref, b_ref, s_ref, o_ref):
    acc = jnp.dot(a_ref[...], w_ref[...], preferred_element_type=F32)
    o_ref[...] = ((acc + b_ref[...]) * s_ref[...]).astype(o_ref.dtype)


def _proj_bias(a, w, b, col_scale, out_dtype):
    m, k = a.shape
    n = w.shape[1]
    bm = _tile(m, 1024, SUBLANE)
    bn = _tile(n, 1024, LANE)
    return pl.pallas_call(
        _bias_kernel,
        grid=(m // bm, n // bn),
        in_specs=[
            pl.BlockSpec((bm, k), lambda i, j: (i, 0)),
            pl.BlockSpec((k, bn), lambda i, j: (0, j)),
            pl.BlockSpec((1, bn), lambda i, j: (0, j)),
            pl.BlockSpec((1, bn), lambda i, j: (0, j)),
        ],
        out_specs=pl.BlockSpec((bm, bn), lambda i, j: (i, j)),
        out_shape=jax.ShapeDtypeStruct((m, n), out_dtype),
        compiler_params=_params("parallel", "arbitrary"),
        name="proj_bias",
    )(a, w, b.reshape(1, n), col_scale.reshape(1, n))


RESIDUAL_OPERAND_BYTES = 48 * 1024 * 1024


def _residual_kernel(a_ref, w_ref, b_ref, g_ref, x_ref, o_ref, *scratch, nk):
    if nk == 1:
        part = jnp.dot(a_ref[...], w_ref[...], preferred_element_type=F32)
        o_ref[...] = x_ref[...] + g_ref[...] * (part + b_ref[...])
        return
    acc_ref, = scratch
    kk = pl.program_id(2)

    @pl.when(kk == 0)
    def _():
        acc_ref[...] = jnp.zeros_like(acc_ref)

    acc_ref[...] += jnp.dot(a_ref[...], w_ref[...], preferred_element_type=F32)

    @pl.when(kk == nk - 1)
    def _():
        o_ref[...] = x_ref[...] + g_ref[...] * (acc_ref[...] + b_ref[...])


def _proj_residual(a, w, b, gate, x, seq_len):
    m, k = a.shape
    n = w.shape[1]
    bn = _tile(n, 512, LANE)
    bm = _tile(seq_len, 1024, SUBLANE)
    while bm > 256 and 2 * 2 * k * (bm + bn) > RESIDUAL_OPERAND_BYTES:
        bm //= 2
    bk = k if 2 * 2 * k * (bm + bn) <= RESIDUAL_OPERAND_BYTES else _tile(k, 4096, LANE)
    nk = k // bk
    per_seq = seq_len // bm
    return pl.pallas_call(
        functools.partial(_residual_kernel, nk=nk),
        grid=(m // bm, n // bn, nk),
        in_specs=[
            pl.BlockSpec((bm, bk), lambda i, j, kk: (i, kk)),
            pl.BlockSpec((bk, bn), lambda i, j, kk: (kk, j)),
            pl.BlockSpec((1, bn), lambda i, j, kk: (0, j)),
            pl.BlockSpec((None, 1, bn), lambda i, j, kk: (i // per_seq, 0, j)),
            pl.BlockSpec((bm, bn), lambda i, j, kk: (i, j)),
        ],
        out_specs=pl.BlockSpec((bm, bn), lambda i, j, kk: (i, j)),
        out_shape=jax.ShapeDtypeStruct((m, n), F32),
        scratch_shapes=[pltpu.VMEM((bm, bn), F32)] if nk > 1 else [],
        input_output_aliases={4: 0},
        compiler_params=_params("parallel", "parallel", "arbitrary"),
        name="proj_residual",
    )(a, w, b.reshape(1, n), gate[:, None, :], x)


def _hy_pre_kernel(u0_ref, u1_ref, uv_ref, c0_ref, c1_ref, cv_ref, x0_ref, z_ref):
    rows = u0_ref.shape[0]
    ridx = lax.broadcasted_iota(jnp.int32, u0_ref.shape, 0)
    first = ridx == 0
    last = ridx == rows - 1

    def conv3(u_ref, c_ref):
        u = u_ref[...]
        prev = jnp.where(first, 0.0, pltpu.roll(u, 1, 0))
        nxt = jnp.where(last, 0.0, pltpu.roll(u, rows - 1, 0))
        return prev * c_ref[0:1, :] + u * c_ref[1:2, :] + nxt * c_ref[2:3, :]

    x0_ref[...] = conv3(u0_ref, c0_ref)
    z_ref[...] = conv3(uv_ref, cv_ref) * conv3(u1_ref, c1_ref)


def _hy_pre(u, conv_w, n_seq, seq_len):
    t, d3 = u.shape
    d = d3 // 3
    dt = LANE
    nd = d // dt
    ublk = lambda part: pl.BlockSpec((seq_len, dt), lambda s, j: (s, part * nd + j))
    cblk = lambda part: pl.BlockSpec((3, dt), lambda s, j: (0, part * nd + j))
    oblk = pl.BlockSpec((seq_len, dt), lambda s, j: (s, j))
    return pl.pallas_call(
        _hy_pre_kernel,
        grid=(n_seq, nd),
        in_specs=[ublk(0), ublk(1), ublk(2), cblk(0), cblk(1), cblk(2)],
        out_specs=[oblk, oblk],
        out_shape=[jax.ShapeDtypeStruct((t, d), F32), jax.ShapeDtypeStruct((t, d), F32)],
        compiler_params=_params("parallel", "parallel"),
        name="hy_pre",
    )(u, u, u, conv_w, conv_w, conv_w)


def _hy_filter_kernel(w0_ref, b0_ref, w1_ref, b1_ref, w2_ref, b2_ref, fr_ref, wo_ref, k_ref,
                      *, seq_len, bands, d_model):
    rows = k_ref.shape[0]
    n = 2 * seq_len
    r = pl.program_id(0) * rows + lax.broadcasted_iota(jnp.int32, (rows, 1), 0)
    m = jnp.where(r < seq_len, r, n - r).astype(F32)
    t = m * (1.0 / (seq_len - 1.0))
    w = (2.0 * math.pi / seq_len) * m
    lane = lax.broadcasted_iota(jnp.int32, (1, LANE), 1)
    band = jnp.where(lane <= bands, lane - 1, lane - 1 - bands).astype(F32)
    f = 1e-4 + band * ((bands - 1 - 1e-4) / (bands - 1))
    ang = w * f
    feat = jnp.where(lane == 0, t,
                     jnp.where(lane <= bands, jnp.cos(ang),
                               jnp.where(lane <= 2 * bands, -jnp.sin(ang), 0.0)))
    fr = fr_ref[...]
    h = jnp.sin(fr * (jnp.dot(feat, w0_ref[...], precision=HIGHEST) + b0_ref[...]))
    h = jnp.sin(fr * (jnp.dot(h, w1_ref[...], precision=HIGHEST) + b1_ref[...]))
    h = jnp.sin(fr * (jnp.dot(h, w2_ref[...], precision=HIGHEST) + b2_ref[...]))
    taps = jnp.dot(h, wo_ref[...], precision=HIGHEST)
    ch = lax.broadcasted_iota(jnp.int32, (1, d_model), 1).astype(F32)
    deltas = jnp.abs(HY_MIN_DECAY + ch * ((HY_MAX_DECAY - HY_MIN_DECAY) / (d_model - 1)))
    k = taps * jnp.exp(-t * deltas)
    k_ref[...] = jnp.where(r == seq_len, 0.0, k)


def _hy_filter(seq_len, f_w0, f_b0, f_w1, f_b1, f_w2, f_b2, f_freq, f_out):
    emb, hid = f_w0.shape
    d = f_out.shape[1] // 2
    bands = (emb - 1) // 2
    assert emb <= LANE and hid <= LANE

    def pad2(a, r, c):
        return jnp.pad(a, ((0, r - a.shape[0]), (0, c - a.shape[1])))

    vec = lambda a: pad2(a.reshape(1, -1), 1, LANE)
    rows = _tile(seq_len, 512, SUBLANE)
    nfwd = seq_len // rows
    full = lambda shape: pl.BlockSpec(shape, lambda i: (0, 0))
    return pl.pallas_call(
        functools.partial(_hy_filter_kernel, seq_len=seq_len, bands=bands, d_model=d),
        grid=(2 * nfwd,),
        in_specs=[
            full((LANE, LANE)), full((1, LANE)),
            full((LANE, LANE)), full((1, LANE)),
            full((LANE, LANE)), full((1, LANE)),
            full((1, LANE)),
            pl.BlockSpec((LANE, d), lambda i: (0, i // nfwd)),
        ],
        out_specs=pl.BlockSpec((rows, d), lambda i: (i, 0)),
        out_shape=jax.ShapeDtypeStruct((2 * seq_len, d), F32),
        compiler_params=_params("parallel"),
        name="hy_filter",
    )(pad2(f_w0, LANE, LANE), vec(f_b0), pad2(f_w1, LANE, LANE), vec(f_b1),
      pad2(f_w2, LANE, LANE), vec(f_b2), vec(f_freq), pad2(f_out, LANE, 2 * d))


def _fft_split(n):
    lg = n.bit_length() - 1
    assert n == 1 << lg
    n1 = 1 << (lg // 2)
    return n1, n // n1


def _dft_tables(n):
    n1, n2 = _fft_split(n)

    def cs(phase, period):
        ang = (2.0 * math.pi / period) * (phase % period).astype(F32)
        return jnp.cos(ang), jnp.sin(ang)

    i1 = jnp.arange(n1, dtype=jnp.int32)
    c1, s1 = cs(i1[:, None] * i1[None, :], n1)
    stage1 = jnp.concatenate([c1, -s1], axis=0)
    stage3 = jnp.concatenate([c1, -s1], axis=1) * (1.0 / n)
    i2 = jnp.arange(n2, dtype=jnp.int32)
    freq = i1[:, None, None] + n1 * i2[None, :, None]
    gc, gs = cs(freq * i2[None, None, :], n)
    fwd = jnp.concatenate([jnp.concatenate([gc, gs], axis=2),
                           jnp.concatenate([-gs, gc], axis=2)], axis=1)
    inv = jnp.swapaxes(fwd, 1, 2)
    return stage1, fwd, inv, stage3


def _dft_dot(f, x):
    return jnp.dot(f, x.astype(BF16), preferred_element_type=F32)


FFT_ROWS = 16


def _dft1_kernel(f_ref, z_ref, o_ref):
    f = f_ref[...]
    zt = jnp.swapaxes(z_ref[...].astype(BF16), 0, 1)
    out = jnp.stack([_dft_dot(f, zt[s]) for s in range(zt.shape[0])], axis=0)
    o_ref[...] = jnp.swapaxes(out.astype(o_ref.dtype), 0, 1)


def _dft_stage1(f, z3, nb):
    rows_all, n2, d = z3.shape
    r = rows_all // nb
    m = f.shape[0]
    st = _tile(n2, FFT_ROWS, FFT_ROWS)
    dt = _tile(d, 512, LANE)
    return pl.pallas_call(
        _dft1_kernel,
        grid=(nb, n2 // st, d // dt),
        in_specs=[
            pl.BlockSpec((m, r), lambda b, i, j: (0, 0)),
            pl.BlockSpec((r, st, dt), lambda b, i, j: (b, i, j)),
        ],
        out_specs=pl.BlockSpec((None, m, st, dt), lambda b, i, j: (b, 0, i, j)),
        out_shape=jax.ShapeDtypeStruct((nb, m, n2, d), BF16),
        compiler_params=_params("parallel", "parallel", "parallel"),
        name="dft_stage1",
    )(f, z3)


def _dft2_filter_kernel(mf_ref, a_ref, o_ref):
    n2 = a_ref.shape[1]
    a = a_ref[...].reshape(2 * n2, a_ref.shape[2])
    o_ref[...] = _dft_dot(mf_ref[...], a).reshape(o_ref.shape)


def _dft_stage2_filter(fwd, a5, dt):
    _, n1, n2, d = a5.shape
    return pl.pallas_call(
        _dft2_filter_kernel,
        grid=(n1, d // dt),
        in_specs=[
            pl.BlockSpec((None, 2 * n2, 2 * n2), lambda k, j: (k, 0, 0)),
            pl.BlockSpec((2, None, n2, dt), lambda k, j: (0, k, 0, j)),
        ],
        out_specs=pl.BlockSpec((2, None, n2, dt), lambda k, j: (0, k, 0, j)),
        out_shape=jax.ShapeDtypeStruct(a5.shape, F32),
        compiler_params=_params("parallel", "parallel"),
        name="dft_stage2_filter",
    )(fwd, a5)


def _dft2_kernel(mf_ref, mi_ref, kf_ref, a_ref, o_ref):
    n2 = a_ref.shape[1]
    a = a_ref[...].reshape(2 * n2, a_ref.shape[2])
    x = _dft_dot(mf_ref[...], a)
    xr, xi = x[:n2], x[n2:]
    kr, ki = kf_ref[0], kf_ref[1]
    y = jnp.concatenate([xr * kr - xi * ki, xr * ki + xi * kr], axis=0)
    o_ref[...] = _dft_dot(mi_ref[...], y).astype(o_ref.dtype).reshape(o_ref.shape)


def _dft_stage2(fwd, inv, kf, a5, dt):
    nb, _, n1, n2, d = a5.shape
    ablk = pl.BlockSpec((None, 2, None, n2, dt), lambda k, j, b: (b, 0, k, 0, j))
    mblk = pl.BlockSpec((None, 2 * n2, 2 * n2), lambda k, j, b: (k, 0, 0))
    return pl.pallas_call(
        _dft2_kernel,
        grid=(n1, d // dt, nb),
        in_specs=[
            mblk, mblk,
            pl.BlockSpec((2, None, n2, dt), lambda k, j, b: (0, k, 0, j)),
            ablk,
        ],
        out_specs=ablk,
        out_shape=jax.ShapeDtypeStruct(a5.shape, BF16),
        compiler_params=_params("parallel", "parallel", "parallel"),
        name="dft_stage2",
    )(fwd, inv, kf, a5)


def _dft3_kernel(f_ref, b_ref, z_ref, x0_ref, d_ref, o_ref):
    f = f_ref[...]
    bt = jnp.swapaxes(b_ref[...], 0, 1)
    y = jnp.stack([_dft_dot(f, bt[s]) for s in range(bt.shape[0])], axis=0)
    y = jnp.swapaxes(y, 0, 1) + z_ref[...] * d_ref[...]
    o_ref[...] = (x0_ref[...] * y).astype(o_ref.dtype)


def _dft_stage3(f, b4, z3, x03, d_skip):
    nb, rows2, n2, d = b4.shape
    r = z3.shape[0] // nb
    st = _tile(n2, FFT_ROWS, FFT_ROWS)
    dt = _tile(d, 512, LANE)
    vblk = pl.BlockSpec((r, st, dt), lambda b, i, j: (b, i, j))
    return pl.pallas_call(
        _dft3_kernel,
        grid=(nb, n2 // st, d // dt),
        in_specs=[
            pl.BlockSpec((r, rows2), lambda b, i, j: (0, 0)),
            pl.BlockSpec((None, rows2, st, dt), lambda b, i, j: (b, 0, i, j)),
            vblk, vblk,
            pl.BlockSpec((1, dt), lambda b, i, j: (0, j)),
        ],
        out_specs=vblk,
        out_shape=jax.ShapeDtypeStruct(z3.shape, BF16),
        compiler_params=_params("parallel", "parallel", "parallel"),
        name="dft_stage3",
    )(f, b4, z3, x03, d_skip.reshape(1, d))


def _hyena_mix(hn, p, j, n_seq, seq_len, tables):
    d = hn.shape[1]
    n1, n2 = _fft_split(2 * seq_len)
    stage1, stage1_half, fwd, inv, stage3_half = tables
    u = _proj_bias(hn, p['hy_w_in'][j], p['hy_b_in'][j], jnp.ones((3 * d,), F32), F32)
    x0, z = _hy_pre(u, p['hy_conv_w'][j], n_seq, seq_len)
    dt = _tile(d, 4096, LANE)
    kfil = _hy_filter(seq_len, p['hy_f_w0'][j], p['hy_f_b0'][j], p['hy_f_w1'][j], p['hy_f_b1'][j],
                      p['hy_f_w2'][j], p['hy_f_b2'][j], p['hy_f_freq'][j], p['hy_f_out'][j])
    ka = _dft_stage1(stage1, kfil.reshape(n1, n2, d), 1)
    kf = _dft_stage2_filter(fwd, ka.reshape(2, n1, n2, d), dt)
    half = n1 // 2
    z3 = z.reshape(n_seq * half, n2, d)
    a = _dft_stage1(stage1_half, z3, n_seq)
    b = _dft_stage2(fwd, inv, kf, a.reshape(n_seq, 2, n1, n2, d), dt)
    out = _dft_stage3(stage3_half, b.reshape(n_seq, 2 * n1, n2, d), z3,
                      x0.reshape(n_seq * half, n2, d), p['hy_d'][j])
    return out.reshape(n_seq * seq_len, d)


def _hyena_tables(seq_len):
    stage1, fwd, inv, stage3 = _dft_tables(2 * seq_len)
    half = stage1.shape[1] // 2
    return tuple(m.astype(BF16) for m in (stage1, stage1[:, :half], fwd, inv, stage3[:half]))


def _na_bias_table(rpb, rows):
    heads, nrh, nrw = rpb.shape
    kh_full, kw = (nrh + 1) // 2, (nrw + 1) // 2
    kh = min(kh_full, rows)
    w = GRID_W
    assert kw <= w and nrw <= 2 * w - 1
    cols = jnp.arange(w)
    col_start = jnp.clip(cols - kw // 2, 0, w - kw)
    valid = (cols[None, :] >= col_start[:, None]) & (cols[None, :] < col_start[:, None] + kw)
    u = jnp.pad(rpb, ((0, 0), (0, 0), (w - kw, 2 * w - (w - kw) - nrw)))
    flat = jnp.tile(u, (1, 1, w))[:, :, w - 1:w - 1 + w * (2 * w - 1)]
    toep = flat.reshape(heads, nrh, w, 2 * w - 1)[:, :, :, :w]
    toep = jnp.where(valid[None, None], toep, MASK_VALUE)
    tab = jnp.stack([toep[:, kh_full - 1 - case:kh_full - 1 - case + kh] for case in range(kh)], axis=1)
    tab = jnp.transpose(tab, (0, 1, 3, 2, 4))
    return tab.reshape(heads, kh, w, kh * w).astype(F32), kh


NA_CHUNK = 32
NA_UNROLL = 8


def _na_kernel(q_ref, k_ref, v_ref, bias_ref, o_ref, s_ref, p_ref, *, rows, kh):
    half = kh // 2
    chunk = s_ref.shape[0] // GRID_W
    keys = kh * GRID_W

    def chunk_body(c, carry):
        r0 = c * chunk

        def scores(i, carry):
            r = r0 + i
            rs = jnp.clip(r - half, 0, rows - kh)
            q = q_ref[pl.ds(pl.multiple_of(r * GRID_W, GRID_W), GRID_W), :]
            ks = k_ref[pl.ds(pl.multiple_of(rs * GRID_W, GRID_W), keys), :]
            s = lax.dot_general(q, ks, (((1,), (1,)), ((), ())), preferred_element_type=F32)
            s_ref[pl.ds(pl.multiple_of(i * GRID_W, GRID_W), GRID_W), :] = s + bias_ref[r - rs]
            return carry

        def softmax(i, carry):
            rows_i = pl.ds(pl.multiple_of(i * GRID_W, GRID_W), GRID_W)
            s = s_ref[rows_i, :]
            e = jnp.exp(s - jnp.max(s, axis=-1, keepdims=True))
            inv = 1.0 / jnp.sum(e, axis=-1, keepdims=True)
            p_ref[rows_i, :] = (e * inv).astype(p_ref.dtype)
            return carry

        def values(i, carry):
            r = r0 + i
            rs = jnp.clip(r - half, 0, rows - kh)
            vs = v_ref[pl.ds(pl.multiple_of(rs * GRID_W, GRID_W), keys), :]
            prob = p_ref[pl.ds(pl.multiple_of(i * GRID_W, GRID_W), GRID_W), :]
            o = jnp.dot(prob, vs, preferred_element_type=F32)
            o_ref[pl.ds(pl.multiple_of(r * GRID_W, GRID_W), GRID_W), :] = o.astype(o_ref.dtype)
            return carry

        lax.fori_loop(0, chunk, scores, 0, unroll=NA_UNROLL)
        lax.fori_loop(0, chunk, softmax, 0, unroll=NA_UNROLL)
        lax.fori_loop(0, chunk, values, 0, unroll=NA_UNROLL)
        return carry

    lax.fori_loop(0, rows // chunk, chunk_body, 0)


def _na_attention(qkv, rpb, n_seq, seq_len):
    t, d3 = qkv.shape
    d = d3 // 3
    heads = rpb.shape[0]
    hd = d // heads
    assert hd % LANE == 0 and seq_len % GRID_W == 0
    rows = seq_len // GRID_W
    bias, kh = _na_bias_table(rpb, rows)
    chunk = _tile(rows, NA_CHUNK, NA_UNROLL)
    blk = lambda part: pl.BlockSpec((seq_len, hd), lambda s, h: (s, part * heads + h))
    return pl.pallas_call(
        functools.partial(_na_kernel, rows=rows, kh=kh),
        grid=(n_seq, heads),
        in_specs=[blk(0), blk(1), blk(2),
                  pl.BlockSpec((None, kh, GRID_W, kh * GRID_W), lambda s, h: (h, 0, 0, 0))],
        out_specs=pl.BlockSpec((seq_len, hd), lambda s, h: (s, h)),
        out_shape=jax.ShapeDtypeStruct((t, d), BF16),
        scratch_shapes=[pltpu.VMEM((chunk * GRID_W, kh * GRID_W), F32),
                        pltpu.VMEM((chunk * GRID_W, kh * GRID_W), BF16)],
        compiler_params=_params("parallel", "parallel"),
        name="na_attention",
    )(qkv, qkv, qkv, bias)


def _na_mix(hn, p, j, n_seq, seq_len):
    d = hn.shape[1]
    heads = p['na_rpb'].shape[1]
    q_scale = (d // heads) ** -0.5
    col_scale = jnp.concatenate([jnp.full((d,), q_scale, F32), jnp.ones((2 * d,), F32)])
    qkv = _proj_bias(hn, p['na_w_qkv'][j], p['na_b_qkv'][j], col_scale, BF16)
    return _na_attention(qkv, p['na_rpb'][j], n_seq, seq_len)


def _ffn(x, hn_args, wg, wu, wd, gate, seq_len):
    hn = _norm_mod(x, *hn_args, seq_len)
    h = _swiglu_up(hn, wg, wu)
    zero_b = jnp.zeros((x.shape[1],), F32)
    return _proj_residual(h, wd, zero_b, 0.5 * gate, x, seq_len)


def kernel(x_prompt, x_sample, c_prompt, c_sample, norm_g, ada_w, ada_b, ffn_w_gate, ffn_w_up, ffn_w_down, hy_w_in, hy_b_in, hy_conv_w, hy_f_w0, hy_f_b0, hy_f_w1, hy_f_b1, hy_f_w2, hy_f_b2, hy_f_freq, hy_f_out, hy_d, hy_w_out, hy_b_out, na_w_qkv, na_b_qkv, na_rpb, na_w_out, na_b_out, final_g):
    nb_p, seq_len, d = x_prompt.shape
    nb_s = x_sample.shape[0]
    assert x_sample.shape[1:] == (seq_len, d)
    n_seq = nb_p + nb_s
    depth = ada_w.shape[0]
    assert ffn_w_gate.shape[-1] % LANE == 0

    x =jnp.concatenate([x_prompt.reshape(nb_p * seq_len, d), x_sample.reshape(nb_s * seq_len, d)], axis=0)
    c_act = jax.nn.silu(jnp.concatenate([c_prompt, c_sample], axis=0))
    mod = _ada_mod(c_act, ada_w, ada_b).reshape(depth, n_seq, ada_w.shape[-1] // d, d)

    wg_all = ffn_w_gate.astype(BF16)
    wu_all = ffn_w_up.astype(BF16)
    wd_all = ffn_w_down.astype(BF16)
    p = dict(hy_w_in=hy_w_in.astype(BF16), hy_b_in=hy_b_in, hy_conv_w=hy_conv_w,
             hy_f_w0=hy_f_w0, hy_f_b0=hy_f_b0, hy_f_w1=hy_f_w1, hy_f_b1=hy_f_b1,
             hy_f_w2=hy_f_w2, hy_f_b2=hy_f_b2, hy_f_freq=hy_f_freq, hy_f_out=hy_f_out,
             hy_d=hy_d, hy_w_out=hy_w_out.astype(BF16), hy_b_out=hy_b_out,
             na_w_qkv=na_w_qkv.astype(BF16), na_b_qkv=na_b_qkv, na_rpb=na_rpb,
             na_w_out=na_w_out.astype(BF16), na_b_out=na_b_out)

    tables = _hyena_tables(seq_len)
    for i in range(depth):
        m = mod[i]
        x = _ffn(x, (norm_g[i, 0], m[:, 1], m[:, 0]), wg_all[i, 0], wu_all[i, 0], wd_all[i, 0], m[:, 2], seq_len)
        hn = _norm_mod(x, norm_g[i, 1], m[:, 4], m[:, 3], seq_len)
        j = i // 2
        if i % 2 == 0:
            mixed = _hyena_mix(hn, p, j, n_seq, seq_len, tables)
            x = _proj_residual(mixed, p['hy_w_out'][j], p['hy_b_out'][j], m[:, 5], x, seq_len)
        else:
            mixed = _na_mix(hn, p, j, n_seq, seq_len)
            x = _proj_residual(mixed, p['na_w_out'][j], p['na_b_out'][j], m[:, 5], x, seq_len)
        x = _ffn(x, (norm_g[i, 2], m[:, 7], m[:, 6]), wg_all[i, 1], wu_all[i, 1], wd_all[i, 1], m[:, 8], seq_len)

    y_prompt = _final_norm(x, final_g, 0, nb_p * seq_len).reshape(nb_p, seq_len, d)
    y_sample = _final_norm(x, final_g, nb_p * seq_len, nb_s * seq_len).reshape(nb_s, seq_len, d)
    return (y_prompt, y_sample)
```

```python
import functools
import math

import jax
import jax.numpy as jnp
from jax import lax
from jax.experimental import pallas as pl
from jax.experimental.pallas import tpu as pltpu

F32 = jnp.float32
BF16 = jnp.bfloat16
HIGHEST = lax.Precision.HIGHEST

EPS = 1e-6
GRID_W = 64
HY_TARGET = 1e-2
HY_MAX_DECAY = math.log(HY_TARGET) / 0.3
HY_MIN_DECAY = math.log(HY_TARGET) / 1.5
MASK_VALUE = -1e30

LANE = 128
SUBLANE = 8
VMEM_LIMIT = 56 * 1024 * 1024


def _params(*sem):
    return pltpu.CompilerParams(dimension_semantics=sem, vmem_limit_bytes=VMEM_LIMIT)


def _tile(n, want, unit):
    if n <= want:
        return n
    t = (want // unit) * unit
    while t > unit and n % t:
        t -= unit
    assert n % t == 0, (n, want, unit)
    return t


def _ada_kernel(cb_ref, w_ref, b_ref, o_ref):
    n_seq = cb_ref.shape[0]
    tn = w_ref.shape[1]
    for t in range(tn // LANE):
        cols = slice(t * LANE, (t + 1) * LANE)
        for s in range(n_seq):
            acc = jnp.sum(w_ref[:, cols] * cb_ref[s], axis=0, keepdims=True)
            o_ref[s:s + 1, cols] = acc + b_ref[:, cols]


def _ada_mod(c_act, ada_w, ada_b):
    depth, k, n = ada_w.shape
    n_seq = c_act.shape[0]
    tn = _tile(n, 512, LANE)
    cb = jnp.broadcast_to(c_act[:, :, None], (n_seq, k, LANE))
    return pl.pallas_call(
        _ada_kernel,
        grid=(depth, n // tn),
        in_specs=[
            pl.BlockSpec((n_seq, k, LANE), lambda l, j: (0, 0, 0)),
            pl.BlockSpec((None, k, tn), lambda l, j: (l, 0, j)),
            pl.BlockSpec((None, 1, tn), lambda l, j: (l, 0, j)),
        ],
        out_specs=pl.BlockSpec((None, n_seq, tn), lambda l, j: (l, 0, j)),
        out_shape=jax.ShapeDtypeStruct((depth, n_seq, n), F32),
        compiler_params=_params("parallel", "parallel"),
        name="ada_mod",
    )(cb, ada_w, ada_b.reshape(depth, 1, n))


def _norm_mod_kernel(x_ref, g_ref, scale_ref, shift_ref, o_ref):
    x = x_ref[...]
    y = x * lax.rsqrt(jnp.mean(x * x, axis=-1, keepdims=True) + EPS)
    h = (y * g_ref[...]) * (1.0 + scale_ref[...]) + shift_ref[...]
    o_ref[...] = h.astype(o_ref.dtype)


def _norm_kernel(x_ref, g_ref, o_ref):
    x = x_ref[...]
    y = x * lax.rsqrt(jnp.mean(x * x, axis=-1, keepdims=True) + EPS)
    o_ref[...] = (y * g_ref[...]).astype(o_ref.dtype)


def _norm_mod(x, g, scale, shift, seq_len):
    t, d = x.shape
    bm = _tile(seq_len, 512, SUBLANE)
    per_seq = seq_len // bm
    vec = pl.BlockSpec((None, 1, d), lambda i: (i // per_seq, 0, 0))
    return pl.pallas_call(
        _norm_mod_kernel,
        grid=(t // bm,),
        in_specs=[
            pl.BlockSpec((bm, d), lambda i: (i, 0)),
            pl.BlockSpec((1, d), lambda i: (0, 0)),
            vec, vec,
        ],
        out_specs=pl.BlockSpec((bm, d), lambda i: (i, 0)),
        out_shape=jax.ShapeDtypeStruct((t, d), BF16),
        compiler_params=_params("parallel"),
        name="norm_mod",
    )(x, g.reshape(1, d), scale[:, None, :], shift[:, None, :])


def _final_norm(x, g, row0, rows):
    _, d = x.shape
    bm = _tile(math.gcd(rows, row0) if row0 else rows, 512, SUBLANE)
    off = row0 // bm
    return pl.pallas_call(
        _norm_kernel,
        grid=(rows // bm,),
        in_specs=[
            pl.BlockSpec((bm, d), lambda i: (i + off, 0)),
            pl.BlockSpec((1, d), lambda i: (0, 0)),
        ],
        out_specs=pl.BlockSpec((bm, d), lambda i: (i, 0)),
        out_shape=jax.ShapeDtypeStruct((rows, d), F32),
        compiler_params=_params("parallel"),
        name="final_norm",
    )(x, g.reshape(1, d))


SWIGLU_TILE_ELEMS = 1024 * 512


def _swiglu_kernel(a_ref, wg_ref, wu_ref, o_ref):
    a = a_ref[...]
    g = jnp.dot(a, wg_ref[...], preferred_element_type=F32)
    u = jnp.dot(a, wu_ref[...], preferred_element_type=F32)
    o_ref[...] = (g * jax.nn.sigmoid(g) * u).astype(o_ref.dtype)


def _swiglu_up(a, wg, wu):
    m, k = a.shape
    n = wg.shape[1]
    bn = next(c for c in (512, 256, LANE) if n % c == 0)
    bm = _tile(m, SWIGLU_TILE_ELEMS // bn, SUBLANE)
    return pl.pallas_call(
        _swiglu_kernel,
        grid=(m // bm, n // bn),
        in_specs=[
            pl.BlockSpec((bm, k), lambda i, j: (i, 0)),
            pl.BlockSpec((k, bn), lambda i, j: (0, j)),
            pl.BlockSpec((k, bn), lambda i, j: (0, j)),
        ],
        out_specs=pl.BlockSpec((bm, bn), lambda i, j: (i, j)),
        out_shape=jax.ShapeDtypeStruct((m, n), BF16),
        compiler_params=_params("parallel", "arbitrary"),
        name="swiglu_up",
    )(a, wg, wu)


def _bias_kernel(a_ref, w_ref, b_ref, s_ref, o_ref):
    acc = jnp.dot(a_ref[...], w_ref[...], preferred_element_type=F32)
    o_ref[...] = ((acc + b_ref[...]) * s_ref[...]).astype(o_ref.dtype)


def _proj_bias(a, w, b, col_scale, out_dtype):
    m, k = a.shape
    n = w.shape[1]
    bm = _tile(m, 1024, SUBLANE)
    bn = _tile(n, 1024, LANE)
    return pl.pallas_call(
        _bias_kernel,
        grid=(m // bm, n // bn),
        in_specs=[
            pl.BlockSpec((bm, k), lambda i, j: (i, 0)),
            pl.BlockSpec((k, bn), lambda i, j: (0, j)),
            pl.BlockSpec((1, bn), lambda i, j: (0, j)),
            pl.BlockSpec((1, bn), lambda i, j: (0, j)),
        ],
        out_specs=pl.BlockSpec((bm, bn), lambda i, j: (i, j)),
        out_shape=jax.ShapeDtypeStruct((m, n), out_dtype),
        compiler_params=_params("parallel", "arbitrary"),
        name="proj_bias",
    )(a, w, b.reshape(1, n), col_scale.reshape(1, n))


RESIDUAL_OPERAND_BYTES = 48 * 1024 * 1024


def _residual_kernel(a_ref, w_ref, b_ref, g_ref, x_ref, o_ref, *scratch, nk):
    if nk == 1:
        part = jnp.dot(a_ref[...], w_ref[...], preferred_element_type=F32)
        o_ref[...] = x_ref[...] + g_ref[...] * (part + b_ref[...])
        return
    acc_ref, = scratch
    kk = pl.program_id(2)

    @pl.when(kk == 0)
    def _():
        acc_ref[...] = jnp.zeros_like(acc_ref)

    acc_ref[...] += jnp.dot(a_ref[...], w_ref[...], preferred_element_type=F32)

    @pl.when(kk == nk - 1)
    def _():
        o_ref[...] = x_ref[...] + g_ref[...] * (acc_ref[...] + b_ref[...])


def _proj_residual(a, w, b, gate, x, seq_len):
    m, k = a.shape
    n = w.shape[1]
    bn = _tile(n, 512, LANE)
    bm = _tile(seq_len, 1024, SUBLANE)
    while bm > 256 and 2 * 2 * k * (bm + bn) > RESIDUAL_OPERAND_BYTES:
        bm //= 2
    bk = k if 2 * 2 * k * (bm + bn) <= RESIDUAL_OPERAND_BYTES else _tile(k, 4096, LANE)
    nk = k // bk
    per_seq = seq_len // bm
    return pl.pallas_call(
        functools.partial(_residual_kernel, nk=nk),
        grid=(m // bm, n // bn, nk),
        in_specs=[
            pl.BlockSpec((bm, bk), lambda i, j, kk: (i, kk)),
            pl.BlockSpec((bk, bn), lambda i, j, kk: (kk, j)),
            pl.BlockSpec((1, bn), lambda i, j, kk: (0, j)),
            pl.BlockSpec((None, 1, bn), lambda i, j, kk: (i // per_seq, 0, j)),
            pl.BlockSpec((bm, bn), lambda i, j, kk: (i, j)),
        ],
        out_specs=pl.BlockSpec((bm, bn), lambda i, j, kk: (i, j)),
        out_shape=jax.ShapeDtypeStruct((m, n), F32),
        scratch_shapes=[pltpu.VMEM((bm, bn), F32)] if nk > 1 else [],
        input_output_aliases={4: 0},
        compiler_params=_params("parallel", "parallel", "arbitrary"),
        name="proj_residual",
    )(a, w, b.reshape(1, n), gate[:, None, :], x)


def _hy_pre_kernel(u0_ref, u1_ref, uv_ref, c0_ref, c1_ref, cv_ref, x0_ref, z_ref):
    rows = u0_ref.shape[0]
    ridx = lax.broadcasted_iota(jnp.int32, u0_ref.shape, 0)
    first = ridx == 0
    last = ridx == rows - 1

    def conv3(u_ref, c_ref):
        u = u_ref[...]
        prev = jnp.where(first, 0.0, pltpu.roll(u, 1, 0))
        nxt = jnp.where(last, 0.0, pltpu.roll(u, rows - 1, 0))
        return prev * c_ref[0:1, :] + u * c_ref[1:2, :] + nxt * c_ref[2:3, :]

    x0_ref[...] = conv3(u0_ref, c0_ref)
    z_ref[...] = conv3(uv_ref, cv_ref) * conv3(u1_ref, c1_ref)


def _hy_pre(u, conv_w, n_seq, seq_len):
    t, d3 = u.shape
    d = d3 // 3
    dt = LANE
    nd = d // dt
    ublk = lambda part: pl.BlockSpec((seq_len, dt), lambda s, j: (s, part * nd + j))
    cblk = lambda part: pl.BlockSpec((3, dt), lambda s, j: (0, part * nd + j))
    oblk = pl.BlockSpec((seq_len, dt), lambda s, j: (s, j))
    return pl.pallas_call(
        _hy_pre_kernel,
        grid=(n_seq, nd),
        in_specs=[ublk(0), ublk(1), ublk(2), cblk(0), cblk(1), cblk(2)],
        out_specs=[oblk, oblk],
        out_shape=[jax.ShapeDtypeStruct((t, d), F32), jax.ShapeDtypeStruct((t, d), F32)],
        compiler_params=_params("parallel", "parallel"),
        name="hy_pre",
    )(u, u, u, conv_w, conv_w, conv_w)


def _hy_filter_kernel(w0_ref, b0_ref, w1_ref, b1_ref, w2_ref, b2_ref, fr_ref, wo_ref, k_ref,
                      *, seq_len, bands, d_model):
    rows = k_ref.shape[0]
    n = 2 * seq_len
    r = pl.program_id(0) * rows + lax.broadcasted_iota(jnp.int32, (rows, 1), 0)
    m = jnp.where(r < seq_len, r, n - r).astype(F32)
    t = m * (1.0 / (seq_len - 1.0))
    w = (2.0 * math.pi / seq_len) * m
    lane = lax.broadcasted_iota(jnp.int32, (1, LANE), 1)
    band = jnp.where(lane <= bands, lane - 1, lane - 1 - bands).astype(F32)
    f = 1e-4 + band * ((bands - 1 - 1e-4) / (bands - 1))
    ang = w * f
    feat = jnp.where(lane == 0, t,
                     jnp.where(lane <= bands, jnp.cos(ang),
                               jnp.where(lane <= 2 * bands, -jnp.sin(ang), 0.0)))
    fr = fr_ref[...]
    h = jnp.sin(fr * (jnp.dot(feat, w0_ref[...], precision=HIGHEST) + b0_ref[...]))
    h = jnp.sin(fr * (jnp.dot(h, w1_ref[...], precision=HIGHEST) + b1_ref[...]))
    h = jnp.sin(fr * (jnp.dot(h, w2_ref[...], precision=HIGHEST) + b2_ref[...]))
    taps = jnp.dot(h, wo_ref[...], precision=HIGHEST)
    ch = lax.broadcasted_iota(jnp.int32, (1, d_model), 1).astype(F32)
    deltas = jnp.abs(HY_MIN_DECAY + ch * ((HY_MAX_DECAY - HY_MIN_DECAY) / (d_model - 1)))
    k = taps * jnp.exp(-t * deltas)
    k_ref[...] = jnp.where(r == seq_len, 0.0, k)


def _hy_filter(seq_len, f_w0, f_b0, f_w1, f_b1, f_w2, f_b2, f_freq, f_out):
    emb, hid = f_w0.shape
    d = f_out.shape[1] // 2
    bands = (emb - 1) // 2
    assert emb <= LANE and hid <= LANE

    def pad2(a, r, c):
        return jnp.pad(a, ((0, r - a.shape[0]), (0, c - a.shape[1])))

    vec = lambda a: pad2(a.reshape(1, -1), 1, LANE)
    rows = _tile(seq_len, 512, SUBLANE)
    nfwd = seq_len // rows
    full = lambda shape: pl.BlockSpec(shape, lambda i: (0, 0))
    return pl.pallas_call(
        functools.partial(_hy_filter_kernel, seq_len=seq_len, bands=bands, d_model=d),
        grid=(2 * nfwd,),
        in_specs=[
            full((LANE, LANE)), full((1, LANE)),
            full((LANE, LANE)), full((1, LANE)),
            full((LANE, LANE)), full((1, LANE)),
            full((1, LANE)),
            pl.BlockSpec((LANE, d), lambda i: (0, i // nfwd)),
        ],
        out_specs=pl.BlockSpec((rows, d), lambda i: (i, 0)),
        out_shape=jax.ShapeDtypeStruct((2 * seq_len, d), F32),
        compiler_params=_params("parallel"),
        name="hy_filter",
    )(pad2(f_w0, LANE, LANE), vec(f_b0), pad2(f_w1, LANE, LANE), vec(f_b1),
      pad2(f_w2, LANE, LANE), vec(f_b2), vec(f_freq), pad2(f_out, LANE, 2 * d))


def _fft_split(n):
    lg = n.bit_length() - 1
    assert n == 1 << lg
    n1 = 1 << (lg // 2)
    return n1, n // n1


def _dft_tables(n):
    n1, n2 = _fft_split(n)

    def cs(phase, period):
        ang = (2.0 * math.pi / period) * (phase % period).astype(F32)
        return jnp.cos(ang), jnp.sin(ang)

    i1 = jnp.arange(n1, dtype=jnp.int32)
    c1, s1 = cs(i1[:, None] * i1[None, :], n1)
    stage1 = jnp.concatenate([c1, -s1], axis=0)
    stage3 = jnp.concatenate([c1, -s1], axis=1) * (1.0 / n)
    i2 = jnp.arange(n2, dtype=jnp.int32)
    tc, ts = cs(i1[:, None] * i2[None, :], n)
    fc, fs = cs(i2[:, None] * i2[None, :], n2)
    gc = tc[:, None, :] * fc[None] - ts[:, None, :] * fs[None]
    gs = ts[:, None, :] * fc[None] + tc[:, None, :] * fs[None]
    fwd = jnp.concatenate([jnp.concatenate([gc, gs], axis=2),
                           jnp.concatenate([-gs, gc], axis=2)], axis=1)
    inv = jnp.swapaxes(fwd, 1, 2)
    return stage1, fwd, inv, stage3


def _dft_dot(f, x):
    return jnp.dot(f, x.astype(BF16), preferred_element_type=F32)


FFT_ROWS = 16


def _dft1_kernel(f_ref, z_ref, o_ref):
    f = f_ref[...]
    zt = jnp.swapaxes(z_ref[...].astype(BF16), 0, 1)
    out = jnp.stack([_dft_dot(f, zt[s]) for s in range(zt.shape[0])], axis=0)
    o_ref[...] = jnp.swapaxes(out.astype(o_ref.dtype), 0, 1)


def _dft_stage1(f, z3, nb, block0=0):
    _, n2, d = z3.shape
    m, r = f.shape
    st = _tile(n2, FFT_ROWS, FFT_ROWS)
    dt = _tile(d, 512, LANE)
    return pl.pallas_call(
        _dft1_kernel,
        grid=(nb, n2 // st, d // dt),
        in_specs=[
            pl.BlockSpec((m, r), lambda b, i, j: (0, 0)),
            pl.BlockSpec((r, st, dt), lambda b, i, j: (b + block0, i, j)),
        ],
        out_specs=pl.BlockSpec((None, m, st, dt), lambda b, i, j: (b, 0, i, j)),
        out_shape=jax.ShapeDtypeStruct((nb, m, n2, d), BF16),
        compiler_params=_params("parallel", "parallel", "parallel"),
        name="dft_stage1",
    )(f, z3)


def _dft2_filter_kernel(mf_ref, a_ref, o_ref):
    n2 = a_ref.shape[1]
    a = a_ref[...].reshape(2 * n2, a_ref.shape[2])
    o_ref[...] = _dft_dot(mf_ref[...], a).reshape(o_ref.shape)


def _dft_stage2_filter(fwd, a5, dt):
    _, n1, n2, d = a5.shape
    return pl.pallas_call(
        _dft2_filter_kernel,
        grid=(n1, d // dt),
        in_specs=[
            pl.BlockSpec((None, 2 * n2, 2 * n2), lambda k, j: (k, 0, 0)),
            pl.BlockSpec((2, None, n2, dt), lambda k, j: (0, k, 0, j)),
        ],
        out_specs=pl.BlockSpec((2, None, n2, dt), lambda k, j: (0, k, 0, j)),
        out_shape=jax.ShapeDtypeStruct(a5.shape, F32),
        compiler_params=_params("parallel", "parallel"),
        name="dft_stage2_filter",
    )(fwd, a5)


def _dft2_kernel(mf_ref, mi_ref, kf_ref, a_ref, o_ref):
    n2 = a_ref.shape[1]
    a = a_ref[...].reshape(2 * n2, a_ref.shape[2])
    x = _dft_dot(mf_ref[...], a)
    xr, xi = x[:n2], x[n2:]
    kr, ki = kf_ref[0], kf_ref[1]
    y = jnp.concatenate([xr * kr - xi * ki, xr * ki + xi * kr], axis=0)
    o_ref[...] = _dft_dot(mi_ref[...], y).astype(o_ref.dtype).reshape(o_ref.shape)


def _dft_stage2(fwd, inv, kf, a5, dt):
    nb, _, n1, n2, d = a5.shape
    ablk = pl.BlockSpec((None, 2, None, n2, dt), lambda k, j, b: (b, 0, k, 0, j))
    mblk = pl.BlockSpec((None, 2 * n2, 2 * n2), lambda k, j, b: (k, 0, 0))
    return pl.pallas_call(
        _dft2_kernel,
        grid=(n1, d // dt, nb),
        in_specs=[
            mblk, mblk,
            pl.BlockSpec((2, None, n2, dt), lambda k, j, b: (0, k, 0, j)),
            ablk,
        ],
        out_specs=ablk,
        out_shape=jax.ShapeDtypeStruct(a5.shape, BF16),
        compiler_params=_params("parallel", "parallel", "parallel"),
        name="dft_stage2",
    )(fwd, inv, kf, a5)


def _dft3_kernel(f_ref, b_ref, z_ref, x0_ref, d_ref, o_ref):
    f = f_ref[...]
    bt = jnp.swapaxes(b_ref[...], 0, 1)
    y = jnp.stack([_dft_dot(f, bt[s]) for s in range(bt.shape[0])], axis=0)
    y = jnp.swapaxes(y, 0, 1) + z_ref[...] * d_ref[...]
    o_ref[...] = (x0_ref[...] * y).astype(o_ref.dtype)


def _dft_stage3(f, b4, z3, x03, d_skip, block0=0):
    nb, rows2, n2, d = b4.shape
    r = f.shape[0]
    st = _tile(n2, FFT_ROWS, FFT_ROWS)
    dt = _tile(d, 512, LANE)
    vblk = pl.BlockSpec((r, st, dt), lambda b, i, j: (b + block0, i, j))
    return pl.pallas_call(
        _dft3_kernel,
        grid=(nb, n2 // st, d // dt),
        in_specs=[
            pl.BlockSpec((r, rows2), lambda b, i, j: (0, 0)),
            pl.BlockSpec((None, rows2, st, dt), lambda b, i, j: (b, 0, i, j)),
            vblk, vblk,
            pl.BlockSpec((1, dt), lambda b, i, j: (0, j)),
        ],
        out_specs=pl.BlockSpec((r, st, dt), lambda b, i, j: (b, i, j)),
        out_shape=jax.ShapeDtypeStruct((nb * r, n2, d), BF16),
        compiler_params=_params("parallel", "parallel", "parallel"),
        name="dft_stage3",
    )(f, b4, z3, x03, d_skip.reshape(1, d))


def _hyena_mix(hn, p, j, n_seq, seq_len, tables):
    d = hn.shape[1]
    n1, n2 = _fft_split(2 * seq_len)
    stage1, stage1_half, stage1_pair, fwd, inv, stage3_half, stage3_pair = tables
    u = _proj_bias(hn, p['hy_w_in'][j], p['hy_b_in'][j], jnp.ones((3 * d,), F32), F32)
    x0, z = _hy_pre(u, p['hy_conv_w'][j], n_seq, seq_len)
    dt = _tile(d, 4096, LANE)
    kfil = _hy_filter(seq_len, p['hy_f_w0'][j], p['hy_f_b0'][j], p['hy_f_w1'][j], p['hy_f_b1'][j],
                      p['hy_f_w2'][j], p['hy_f_b2'][j], p['hy_f_freq'][j], p['hy_f_out'][j])
    ka = _dft_stage1(stage1, kfil.reshape(n1, n2, d), 1)
    kf = _dft_stage2_filter(fwd, ka.reshape(2, n1, n2, d), dt)
    half = n1 // 2
    z3 = z.reshape(n_seq * half, n2, d)
    x03 = x0.reshape(n_seq * half, n2, d)
    outs = []
    n_pairs = n_seq // 2
    if n_pairs:
        a = _dft_stage1(stage1_pair, z3, n_pairs)
        b = _dft_stage2(fwd, inv, kf, a.reshape(n_pairs, 2, n1, n2, d), dt)
        outs.append(_dft_stage3(stage3_pair, b.reshape(n_pairs, 2 * n1, n2, d), z3, x03, p['hy_d'][j]))
    if n_seq % 2:
        a = _dft_stage1(stage1_half, z3, 1, block0=n_seq - 1)
        b = _dft_stage2(fwd, inv, kf, a.reshape(1, 2, n1, n2, d), dt)
        outs.append(_dft_stage3(stage3_half, b.reshape(1, 2 * n1, n2, d), z3, x03, p['hy_d'][j],
                                block0=n_seq - 1))
    out = outs[0] if len(outs) == 1 else jnp.concatenate(outs, axis=0)
    return out.reshape(n_seq * seq_len, d)


def _hyena_tables(seq_len):
    stage1, fwd, inv, stage3 = _dft_tables(2 * seq_len)
    n1 = stage1.shape[1]
    half = n1 // 2
    c, ms = stage1[:n1, :half], stage1[n1:, :half]
    stage1_pair = jnp.concatenate([jnp.concatenate([c, -ms], axis=1),
                                   jnp.concatenate([ms, c], axis=1)], axis=0)
    c3, ms3 = stage3[:half, :n1], stage3[:half, n1:]
    stage3_pair = jnp.concatenate([jnp.concatenate([c3, ms3], axis=1),
                                   jnp.concatenate([-ms3, c3], axis=1)], axis=0)
    return tuple(m.astype(BF16) for m in
                 (stage1, stage1[:, :half], stage1_pair, fwd, inv, stage3[:half], stage3_pair))


def _na_bias_table(rpb, rows):
    heads, nrh, nrw = rpb.shape
    kh_full, kw = (nrh + 1) // 2, (nrw + 1) // 2
    kh = min(kh_full, rows)
    w = GRID_W
    assert kw <= w and nrw <= 2 * w - 1
    cols = jnp.arange(w)
    col_start = jnp.clip(cols - kw // 2, 0, w - kw)
    valid = (cols[None, :] >= col_start[:, None]) & (cols[None, :] < col_start[:, None] + kw)
    u = jnp.pad(rpb, ((0, 0), (0, 0), (w - kw, 2 * w - (w - kw) - nrw)))
    flat = jnp.tile(u, (1, 1, w))[:, :, w - 1:w - 1 + w * (2 * w - 1)]
    toep = flat.reshape(heads, nrh, w, 2 * w - 1)[:, :, :, :w]
    toep = jnp.where(valid[None, None], toep, MASK_VALUE)
    tab = jnp.stack([toep[:, kh_full - 1 - case:kh_full - 1 - case + kh] for case in range(kh)], axis=1)
    tab = jnp.transpose(tab, (0, 1, 3, 2, 4))
    return tab.reshape(heads, kh, w, kh * w).astype(F32), kh


NA_CHUNK = 32
NA_UNROLL = 8
NA_MXU_UNROLL = 16


def _na_kernel(q_ref, k_ref, v_ref, bias_ref, o_ref, s_ref, p_ref, *, rows, kh):
    half = kh // 2
    chunk = s_ref.shape[0] // GRID_W
    keys = kh * GRID_W

    def chunk_body(c, carry):
        r0 = c * chunk

        def scores(i, carry):
            r = r0 + i
            rs = jnp.clip(r - half, 0, rows - kh)
            q = q_ref[pl.ds(pl.multiple_of(r * GRID_W, GRID_W), GRID_W), :]
            ks = k_ref[pl.ds(pl.multiple_of(rs * GRID_W, GRID_W), keys), :]
            s = lax.dot_general(q, ks, (((1,), (1,)), ((), ())), preferred_element_type=F32)
            s_ref[pl.ds(pl.multiple_of(i * GRID_W, GRID_W), GRID_W), :] = s + bias_ref[r - rs]
            return carry

        def softmax(i, carry):
            rows_i = pl.ds(pl.multiple_of(i * GRID_W, GRID_W), GRID_W)
            s = s_ref[rows_i, :]
            e = jnp.exp(s - jnp.max(s, axis=-1, keepdims=True))
            inv = 1.0 / jnp.sum(e, axis=-1, keepdims=True)
            p_ref[rows_i, :] = (e * inv).astype(p_ref.dtype)
            return carry

        def values(i, carry):
            r = r0 + i
            rs = jnp.clip(r - half, 0, rows - kh)
            vs = v_ref[pl.ds(pl.multiple_of(rs * GRID_W, GRID_W), keys), :]
            prob = p_ref[pl.ds(pl.multiple_of(i * GRID_W, GRID_W), GRID_W), :]
            o = jnp.dot(prob, vs, preferred_element_type=F32)
            o_ref[pl.ds(pl.multiple_of(r * GRID_W, GRID_W), GRID_W), :] = o.astype(o_ref.dtype)
            return carry

        lax.fori_loop(0, chunk, scores, 0, unroll=min(NA_MXU_UNROLL, chunk))
        lax.fori_loop(0, chunk, softmax, 0, unroll=NA_UNROLL)
        lax.fori_loop(0, chunk, values, 0, unroll=min(NA_MXU_UNROLL, chunk))
        return carry

    lax.fori_loop(0, rows // chunk, chunk_body, 0)


def _na_attention(qkv, rpb, n_seq, seq_len):
    t, d3 = qkv.shape
    d = d3 // 3
    heads = rpb.shape[0]
    hd = d // heads
    assert hd % LANE == 0 and seq_len % GRID_W == 0
    rows = seq_len // GRID_W
    bias, kh = _na_bias_table(rpb, rows)
    chunk = _tile(rows, NA_CHUNK, NA_MXU_UNROLL)
    blk = lambda part: pl.BlockSpec((seq_len, hd), lambda s, h: (s, part * heads + h))
    return pl.pallas_call(
        functools.partial(_na_kernel, rows=rows, kh=kh),
        grid=(n_seq, heads),
        in_specs=[blk(0), blk(1), blk(2),
                  pl.BlockSpec((None, kh, GRID_W, kh * GRID_W), lambda s, h: (h, 0, 0, 0))],
        out_specs=pl.BlockSpec((seq_len, hd), lambda s, h: (s, h)),
        out_shape=jax.ShapeDtypeStruct((t, d), BF16),
        scratch_shapes=[pltpu.VMEM((chunk * GRID_W, kh * GRID_W), F32),
                        pltpu.VMEM((chunk * GRID_W, kh * GRID_W), BF16)],
        compiler_params=_params("parallel", "parallel"),
        name="na_attention",
    )(qkv, qkv, qkv, bias)


def _na_mix(hn, p, j, n_seq, seq_len):
    d = hn.shape[1]
    heads = p['na_rpb'].shape[1]
    q_scale = (d // heads) ** -0.5
    col_scale = jnp.concatenate([jnp.full((d,), q_scale, F32), jnp.ones((2 * d,), F32)])
    qkv = _proj_bias(hn, p['na_w_qkv'][j], p['na_b_qkv'][j], col_scale, BF16)
    return _na_attention(qkv, p['na_rpb'][j], n_seq, seq_len)


def _ffn(x, hn_args, wg, wu, wd, gate, seq_len):
    hn = _norm_mod(x, *hn_args, seq_len)
    h = _swiglu_up(hn, wg, wu)
    zero_b = jnp.zeros((x.shape[1],), F32)
    return _proj_residual(h, wd, zero_b, 0.5 * gate, x, seq_len)


def kernel(x_prompt, x_sample, c_prompt, c_sample, norm_g, ada_w, ada_b, ffn_w_gate, ffn_w_up, ffn_w_down, hy_w_in, hy_b_in, hy_conv_w, hy_f_w0, hy_f_b0, hy_f_w1, hy_f_b1, hy_f_w2, hy_f_b2, hy_f_freq, hy_f_out, hy_d, hy_w_out, hy_b_out, na_w_qkv, na_b_qkv, na_rpb, na_w_out, na_b_out, final_g):
    nb_p, seq_len, d = x_prompt.shape
    nb_s = x_sample.shape[0]
    assert x_sample.shape[1:] == (seq_len, d)
    n_seq = nb_p + nb_s
    depth = ada_w.shape[0]
    assert ffn_w_gate.shape[-1] % LANE == 0

    x =jnp.concatenate([x_prompt.reshape(nb_p * seq_len, d), x_sample.reshape(nb_s * seq_len, d)], axis=0)
    c_act = jax.nn.silu(jnp.concatenate([c_prompt, c_sample], axis=0))
    mod = _ada_mod(c_act, ada_w, ada_b).reshape(depth, n_seq, ada_w.shape[-1] // d, d)

    wg_all = ffn_w_gate.astype(BF16)
    wu_all = ffn_w_up.astype(BF16)
    wd_all = ffn_w_down.astype(BF16)
    p = dict(hy_w_in=hy_w_in.astype(BF16), hy_b_in=hy_b_in, hy_conv_w=hy_conv_w,
             hy_f_w0=hy_f_w0, hy_f_b0=hy_f_b0, hy_f_w1=hy_f_w1, hy_f_b1=hy_f_b1,
             hy_f_w2=hy_f_w2, hy_f_b2=hy_f_b2, hy_f_freq=hy_f_freq, hy_f_out=hy_f_out,
             hy_d=hy_d, hy_w_out=hy_w_out.astype(BF16), hy_b_out=hy_b_out,
             na_w_qkv=na_w_qkv.astype(BF16), na_b_qkv=na_b_qkv, na_rpb=na_rpb,
             na_w_out=na_w_out.astype(BF16), na_b_out=na_b_out)

    tables = _hyena_tables(seq_len)
    for i in range(depth):
        m = mod[i]
        x = _ffn(x, (norm_g[i, 0], m[:, 1], m[:, 0]), wg_all[i, 0], wu_all[i, 0], wd_all[i, 0], m[:, 2], seq_len)
        hn = _norm_mod(x, norm_g[i, 1], m[:, 4], m[:, 3], seq_len)
        j = i // 2
        if i % 2 == 0:
            mixed = _hyena_mix(hn, p, j, n_seq, seq_len, tables)
            x = _proj_residual(mixed, p['hy_w_out'][j], p['hy_b_out'][j], m[:, 5], x, seq_len)
        else:
            mixed = _na_mix(hn, p, j, n_seq, seq_len)
            x = _proj_residual(mixed, p['na_w_out'][j], p['na_b_out'][j], m[:, 5], x, seq_len)
        x = _ffn(x, (norm_g[i, 2], m[:, 7], m[:, 6]), wg_all[i, 1], wu_all[i, 1], wd_all[i, 1], m[:, 8], seq_len)

    y_prompt = _final_norm(x, final_g, 0, nb_p * seq_len).reshape(nb_p, seq_len, d)
    y_sample = _final_norm(x, final_g, nb_p * seq_len, nb_s * seq_len).reshape(nb_s, seq_len, d)
    return (y_prompt, y_sample)
```

```python
import functools
import math

import jax
import jax.numpy as jnp
from jax import lax
from jax.experimental import pallas as pl
from jax.experimental.pallas import tpu as pltpu

F32 = jnp.float32
BF16 = jnp.bfloat16
HIGHEST = lax.Precision.HIGHEST

EPS = 1e-6
GRID_W = 64
HY_TARGET = 1e-2
HY_MAX_DECAY = math.log(HY_TARGET) / 0.3
HY_MIN_DECAY = math.log(HY_TARGET) / 1.5
MASK_VALUE = -1e30

LANE = 128
SUBLANE = 8
VMEM_LIMIT = 56 * 1024 * 1024


def _params(*sem):
    return pltpu.CompilerParams(dimension_semantics=sem, vmem_limit_bytes=VMEM_LIMIT)


def _tile(n, want, unit):
    if n <= want:
        return n
    t = (want // unit) * unit
    while t > unit and n % t:
        t -= unit
    assert n % t == 0, (n, want, unit)
    return t


def _ada_kernel(cb_ref, w_ref, b_ref, o_ref):
    n_seq = cb_ref.shape[0]
    tn = w_ref.shape[1]
    for t in range(tn // LANE):
        cols = slice(t * LANE, (t + 1) * LANE)
        for s in range(n_seq):
            acc = jnp.sum(w_ref[:, cols] * cb_ref[s], axis=0, keepdims=True)
            o_ref[s:s + 1, cols] = acc + b_ref[:, cols]


def _ada_mod(c_act, ada_w, ada_b):
    depth, k, n = ada_w.shape
    n_seq = c_act.shape[0]
    tn = _tile(n, 512, LANE)
    cb = jnp.broadcast_to(c_act[:, :, None], (n_seq, k, LANE))
    return pl.pallas_call(
        _ada_kernel,
        grid=(depth, n // tn),
        in_specs=[
            pl.BlockSpec((n_seq, k, LANE), lambda l, j: (0, 0, 0)),
            pl.BlockSpec((None, k, tn), lambda l, j: (l, 0, j)),
            pl.BlockSpec((None, 1, tn), lambda l, j: (l, 0, j)),
        ],
        out_specs=pl.BlockSpec((None, n_seq, tn), lambda l, j: (l, 0, j)),
        out_shape=jax.ShapeDtypeStruct((depth, n_seq, n), F32),
        compiler_params=_params("parallel", "parallel"),
        name="ada_mod",
    )(cb, ada_w, ada_b.reshape(depth, 1, n))


def _norm_mod_kernel(x_ref, g_ref, scale_ref, shift_ref, o_ref):
    x = x_ref[...]
    y = x * lax.rsqrt(jnp.mean(x * x, axis=-1, keepdims=True) + EPS)
    h = (y * g_ref[...]) * (1.0 + scale_ref[...]) + shift_ref[...]
    o_ref[...] = h.astype(o_ref.dtype)


def _norm_kernel(x_ref, g_ref, o_ref):
    x = x_ref[...]
    y = x * lax.rsqrt(jnp.mean(x * x, axis=-1, keepdims=True) + EPS)
    o_ref[...] = (y * g_ref[...]).astype(o_ref.dtype)


def _norm_mod(x, g, scale, shift, seq_len):
    t, d = x.shape
    bm = _tile(seq_len, 512, SUBLANE)
    per_seq = seq_len // bm
    vec = pl.BlockSpec((None, 1, d), lambda i: (i // per_seq, 0, 0))
    return pl.pallas_call(
        _norm_mod_kernel,
        grid=(t // bm,),
        in_specs=[
            pl.BlockSpec((bm, d), lambda i: (i, 0)),
            pl.BlockSpec((1, d), lambda i: (0, 0)),
            vec, vec,
        ],
        out_specs=pl.BlockSpec((bm, d), lambda i: (i, 0)),
        out_shape=jax.ShapeDtypeStruct((t, d), BF16),
        compiler_params=_params("parallel"),
        name="norm_mod",
    )(x, g.reshape(1, d), scale[:, None, :], shift[:, None, :])


def _final_norm(x, g, row0, rows):
    _, d = x.shape
    bm = _tile(math.gcd(rows, row0) if row0 else rows, 512, SUBLANE)
    off = row0 // bm
    return pl.pallas_call(
        _norm_kernel,
        grid=(rows // bm,),
        in_specs=[
            pl.BlockSpec((bm, d), lambda i: (i + off, 0)),
            pl.BlockSpec((1, d), lambda i: (0, 0)),
        ],
        out_specs=pl.BlockSpec((bm, d), lambda i: (i, 0)),
        out_shape=jax.ShapeDtypeStruct((rows, d), F32),
        compiler_params=_params("parallel"),
        name="final_norm",
    )(x, g.reshape(1, d))


def _wspec(wsel, block, index_fn):
    _, lead = wsel
    return pl.BlockSpec((None,) * len(lead) + block, lambda *g: lead + index_fn(*g))


SWIGLU_TILE_ELEMS = 1024 * 512


def _swiglu_kernel(a_ref, wg_ref, wu_ref, o_ref):
    a = a_ref[...]
    g = jnp.dot(a, wg_ref[...], preferred_element_type=F32)
    u = jnp.dot(a, wu_ref[...], preferred_element_type=F32)
    o_ref[...] = (g * jax.nn.sigmoid(g) * u).astype(o_ref.dtype)


def _swiglu_up(a, wg, wu):
    m, k = a.shape
    n = wg[0].shape[-1]
    bn = next(c for c in (512, 256, LANE) if n % c == 0)
    bm = _tile(m, SWIGLU_TILE_ELEMS // bn, SUBLANE)
    return pl.pallas_call(
        _swiglu_kernel,
        grid=(m // bm, n // bn),
        in_specs=[
            pl.BlockSpec((bm, k), lambda i, j: (i, 0)),
            _wspec(wg, (k, bn), lambda i, j: (0, j)),
            _wspec(wu, (k, bn), lambda i, j: (0, j)),
        ],
        out_specs=pl.BlockSpec((bm, bn), lambda i, j: (i, j)),
        out_shape=jax.ShapeDtypeStruct((m, n), BF16),
        compiler_params=_params("parallel", "arbitrary"),
        name="swiglu_up",
    )(a, wg[0], wu[0])


def _bias_kernel(a_ref, w_ref, b_ref, s_ref, o_ref):
    acc = jnp.dot(a_ref[...], w_ref[...], preferred_element_type=F32)
    o_ref[...] = ((acc + b_ref[...]) * s_ref[...]).astype(o_ref.dtype)


def _proj_bias(a, w, b, col_scale, out_dtype):
    m, k = a.shape
    n = w[0].shape[-1]
    bm = _tile(m, 1024, SUBLANE)
    bn = _tile(n, 1024, LANE)
    return pl.pallas_call(
        _bias_kernel,
        grid=(m // bm, n // bn),
        in_specs=[
            pl.BlockSpec((bm, k), lambda i, j: (i, 0)),
            _wspec(w, (k, bn), lambda i, j: (0, j)),
            pl.BlockSpec((1, bn), lambda i, j: (0, j)),
            pl.BlockSpec((1, bn), lambda i, j: (0, j)),
        ],
        out_specs=pl.BlockSpec((bm, bn), lambda i, j: (i, j)),
        out_shape=jax.ShapeDtypeStruct((m, n), out_dtype),
        compiler_params=_params("parallel", "arbitrary"),
        name="proj_bias",
    )(a, w[0], b.reshape(1, n), col_scale.reshape(1, n))


RESIDUAL_OPERAND_BYTES = 48 * 1024 * 1024


def _residual_kernel(a_ref, w_ref, b_ref, g_ref, x_ref, o_ref, *scratch, nk):
    if nk == 1:
        part = jnp.dot(a_ref[...], w_ref[...], preferred_element_type=F32)
        o_ref[...] = x_ref[...] + g_ref[...] * (part + b_ref[...])
        return
    acc_ref, = scratch
    kk = pl.program_id(2)

    @pl.when(kk == 0)
    def _():
        acc_ref[...] = jnp.zeros_like(acc_ref)

    acc_ref[...] += jnp.dot(a_ref[...], w_ref[...], preferred_element_type=F32)

    @pl.when(kk == nk - 1)
    def _():
        o_ref[...] = x_ref[...] + g_ref[...] * (acc_ref[...] + b_ref[...])


def _proj_residual(a, w, b, gate, x, seq_len):
    m, k = a.shape
    n = w[0].shape[-1]
    bn = _tile(n, 512, LANE)
    bm = _tile(seq_len, 1024, SUBLANE)
    while bm > 256 and 2 * 2 * k * (bm + bn) > RESIDUAL_OPERAND_BYTES:
        bm //= 2
    bk = k if 2 * 2 * k * (bm + bn) <= RESIDUAL_OPERAND_BYTES else _tile(k, 4096, LANE)
    nk = k // bk
    per_seq = seq_len // bm
    return pl.pallas_call(
        functools.partial(_residual_kernel, nk=nk),
        grid=(m // bm, n // bn, nk),
        in_specs=[
            pl.BlockSpec((bm, bk), lambda i, j, kk: (i, kk)),
            _wspec(w, (bk, bn), lambda i, j, kk: (kk, j)),
            pl.BlockSpec((1, bn), lambda i, j, kk: (0, j)),
            pl.BlockSpec((None, 1, bn), lambda i, j, kk: (i // per_seq, 0, j)),
            pl.BlockSpec((bm, bn), lambda i, j, kk: (i, j)),
        ],
        out_specs=pl.BlockSpec((bm, bn), lambda i, j, kk: (i, j)),
        out_shape=jax.ShapeDtypeStruct((m, n), F32),
        scratch_shapes=[pltpu.VMEM((bm, bn), F32)] if nk > 1 else [],
        input_output_aliases={4: 0},
        compiler_params=_params("parallel", "parallel", "arbitrary"),
        name="proj_residual",
    )(a, w[0], b.reshape(1, n), gate[:, None, :], x)


def _hy_pre_kernel(u0_ref, u1_ref, uv_ref, c0_ref, c1_ref, cv_ref, x0_ref, z_ref):
    rows = u0_ref.shape[0]
    ridx = lax.broadcasted_iota(jnp.int32, u0_ref.shape, 0)
    first = ridx == 0
    last = ridx == rows - 1

    def conv3(u_ref, c_ref):
        u = u_ref[...]
        prev = jnp.where(first, 0.0, pltpu.roll(u, 1, 0))
        nxt = jnp.where(last, 0.0, pltpu.roll(u, rows - 1, 0))
        return prev * c_ref[0:1, :] + u * c_ref[1:2, :] + nxt * c_ref[2:3, :]

    x0_ref[...] = conv3(u0_ref, c0_ref)
    z_ref[...] = conv3(uv_ref, cv_ref) * conv3(u1_ref, c1_ref)


def _hy_pre(u, conv_w, n_seq, seq_len):
    t, d3 = u.shape
    d = d3 // 3
    dt = LANE
    nd = d // dt
    ublk = lambda part: pl.BlockSpec((seq_len, dt), lambda s, j: (s, part * nd + j))
    cblk = lambda part: pl.BlockSpec((3, dt), lambda s, j: (0, part * nd + j))
    oblk = pl.BlockSpec((seq_len, dt), lambda s, j: (s, j))
    return pl.pallas_call(
        _hy_pre_kernel,
        grid=(n_seq, nd),
        in_specs=[ublk(0), ublk(1), ublk(2), cblk(0), cblk(1), cblk(2)],
        out_specs=[oblk, oblk],
        out_shape=[jax.ShapeDtypeStruct((t, d), F32), jax.ShapeDtypeStruct((t, d), F32)],
        compiler_params=_params("parallel", "parallel"),
        name="hy_pre",
    )(u, u, u, conv_w, conv_w, conv_w)


def _hy_filter_kernel(w0_ref, b0_ref, w1_ref, b1_ref, w2_ref, b2_ref, fr_ref, wo_ref, k_ref,
                      *, seq_len, bands, d_model):
    rows = k_ref.shape[0]
    n = 2 * seq_len
    r = pl.program_id(0) * rows + lax.broadcasted_iota(jnp.int32, (rows, 1), 0)
    m = jnp.where(r < seq_len, r, n - r).astype(F32)
    t = m * (1.0 / (seq_len - 1.0))
    w = (2.0 * math.pi / seq_len) * m
    lane = lax.broadcasted_iota(jnp.int32, (1, LANE), 1)
    band = jnp.where(lane <= bands, lane - 1, lane - 1 - bands).astype(F32)
    f = 1e-4 + band * ((bands - 1 - 1e-4) / (bands - 1))
    ang = w * f
    feat = jnp.where(lane == 0, t,
                     jnp.where(lane <= bands, jnp.cos(ang),
                               jnp.where(lane <= 2 * bands, -jnp.sin(ang), 0.0)))
    fr = fr_ref[...]
    h = jnp.sin(fr * (jnp.dot(feat, w0_ref[...], precision=HIGHEST) + b0_ref[...]))
    h = jnp.sin(fr * (jnp.dot(h, w1_ref[...], precision=HIGHEST) + b1_ref[...]))
    h = jnp.sin(fr * (jnp.dot(h, w2_ref[...], precision=HIGHEST) + b2_ref[...]))
    taps = jnp.dot(h, wo_ref[...], precision=HIGHEST)
    ch = lax.broadcasted_iota(jnp.int32, (1, d_model), 1).astype(F32)
    deltas = jnp.abs(HY_MIN_DECAY + ch * ((HY_MAX_DECAY - HY_MIN_DECAY) / (d_model - 1)))
    k = taps * jnp.exp(-t * deltas)
    k_ref[...] = jnp.where(r == seq_len, 0.0, k)


def _hy_filter(seq_len, f_w0, f_b0, f_w1, f_b1, f_w2, f_b2, f_freq, f_out):
    emb, hid = f_w0.shape
    d = f_out.shape[1] // 2
    bands = (emb - 1) // 2
    assert emb <= LANE and hid <= LANE

    def pad2(a, r, c):
        return jnp.pad(a, ((0, r - a.shape[0]), (0, c - a.shape[1])))

    vec = lambda a: pad2(a.reshape(1, -1), 1, LANE)
    rows = _tile(seq_len, 512, SUBLANE)
    nfwd = seq_len // rows
    full = lambda shape: pl.BlockSpec(shape, lambda i: (0, 0))
    return pl.pallas_call(
        functools.partial(_hy_filter_kernel, seq_len=seq_len, bands=bands, d_model=d),
        grid=(2 * nfwd,),
        in_specs=[
            full((LANE, LANE)), full((1, LANE)),
            full((LANE, LANE)), full((1, LANE)),
            full((LANE, LANE)), full((1, LANE)),
            full((1, LANE)),
            pl.BlockSpec((LANE, d), lambda i: (0, i // nfwd)),
        ],
        out_specs=pl.BlockSpec((rows, d), lambda i: (i, 0)),
        out_shape=jax.ShapeDtypeStruct((2 * seq_len, d), F32),
        compiler_params=_params("parallel"),
        name="hy_filter",
    )(pad2(f_w0, LANE, LANE), vec(f_b0), pad2(f_w1, LANE, LANE), vec(f_b1),
      pad2(f_w2, LANE, LANE), vec(f_b2), vec(f_freq), pad2(f_out, LANE, 2 * d))


def _fft_split(n):
    lg = n.bit_length() - 1
    assert n == 1 << lg
    n1 = 1 << (lg // 2)
    return n1, n // n1


def _dft_tables(n):
    n1, n2 = _fft_split(n)

    def cs(phase, period):
        ang = (2.0 * math.pi / period) * (phase % period).astype(F32)
        return jnp.cos(ang), jnp.sin(ang)

    i1 = jnp.arange(n1, dtype=jnp.int32)
    c1, s1 = cs(i1[:, None] * i1[None, :], n1)
    stage1 = jnp.concatenate([c1, -s1], axis=0)
    stage3 = jnp.concatenate([c1, -s1], axis=1) * (1.0 / n)
    i2 = jnp.arange(n2, dtype=jnp.int32)
    tc, ts = cs(i1[:, None] * i2[None, :], n)
    fc, fs = cs(i2[:, None] * i2[None, :], n2)
    gc = tc[:, None, :] * fc[None] - ts[:, None, :] * fs[None]
    gs = ts[:, None, :] * fc[None] + tc[:, None, :] * fs[None]
    fwd = jnp.concatenate([jnp.concatenate([gc, gs], axis=2),
                           jnp.concatenate([-gs, gc], axis=2)], axis=1)
    inv = jnp.swapaxes(fwd, 1, 2)
    return stage1, fwd, inv, stage3


def _dft_dot(f, x):
    return jnp.dot(f, x.astype(BF16), preferred_element_type=F32)


FFT_ROWS = 16


def _dft1_kernel(f_ref, z_ref, o_ref):
    f = f_ref[...]
    zt = jnp.swapaxes(z_ref[...].astype(BF16), 0, 1)
    out = jnp.stack([_dft_dot(f, zt[s]) for s in range(zt.shape[0])], axis=0)
    o_ref[...] = jnp.swapaxes(out.astype(o_ref.dtype), 0, 1)


def _dft_stage1(f, z3, nb, block0=0):
    _, n2, d = z3.shape
    m, r = f.shape
    st = _tile(n2, FFT_ROWS, FFT_ROWS)
    dt = _tile(d, 512, LANE)
    return pl.pallas_call(
        _dft1_kernel,
        grid=(nb, n2 // st, d // dt),
        in_specs=[
            pl.BlockSpec((m, r), lambda b, i, j: (0, 0)),
            pl.BlockSpec((r, st, dt), lambda b, i, j: (b + block0, i, j)),
        ],
        out_specs=pl.BlockSpec((None, m, st, dt), lambda b, i, j: (b, 0, i, j)),
        out_shape=jax.ShapeDtypeStruct((nb, m, n2, d), BF16),
        compiler_params=_params("parallel", "parallel", "parallel"),
        name="dft_stage1",
    )(f, z3)


def _dft2_filter_kernel(mf_ref, a_ref, o_ref):
    n2 = a_ref.shape[1]
    a = a_ref[...].reshape(2 * n2, a_ref.shape[2])
    o_ref[...] = _dft_dot(mf_ref[...], a).reshape(o_ref.shape)


def _dft_stage2_filter(fwd, a5, dt):
    _, n1, n2, d = a5.shape
    return pl.pallas_call(
        _dft2_filter_kernel,
        grid=(n1, d // dt),
        in_specs=[
            pl.BlockSpec((None, 2 * n2, 2 * n2), lambda k, j: (k, 0, 0)),
            pl.BlockSpec((2, None, n2, dt), lambda k, j: (0, k, 0, j)),
        ],
        out_specs=pl.BlockSpec((2, None, n2, dt), lambda k, j: (0, k, 0, j)),
        out_shape=jax.ShapeDtypeStruct(a5.shape, F32),
        compiler_params=_params("parallel", "parallel"),
        name="dft_stage2_filter",
    )(fwd, a5)


def _dft2_kernel(mf_ref, mi_ref, kf_ref, a_ref, o_ref):
    n2 = a_ref.shape[1]
    a = a_ref[...].reshape(2 * n2, a_ref.shape[2])
    x = _dft_dot(mf_ref[...], a)
    xr, xi = x[:n2], x[n2:]
    kr, ki = kf_ref[0], kf_ref[1]
    y = jnp.concatenate([xr * kr - xi * ki, xr * ki + xi * kr], axis=0)
    o_ref[...] = _dft_dot(mi_ref[...], y).astype(o_ref.dtype).reshape(o_ref.shape)


def _dft_stage2(fwd, inv, kf, a5, dt):
    nb, _, n1, n2, d = a5.shape
    ablk = pl.BlockSpec((None, 2, None, n2, dt), lambda k, j, b: (b, 0, k, 0, j))
    mblk = pl.BlockSpec((None, 2 * n2, 2 * n2), lambda k, j, b: (k, 0, 0))
    return pl.pallas_call(
        _dft2_kernel,
        grid=(n1, d // dt, nb),
        in_specs=[
            mblk, mblk,
            pl.BlockSpec((2, None, n2, dt), lambda k, j, b: (0, k, 0, j)),
            ablk,
        ],
        out_specs=ablk,
        out_shape=jax.ShapeDtypeStruct(a5.shape, BF16),
        compiler_params=_params("parallel", "parallel", "parallel"),
        name="dft_stage2",
    )(fwd, inv, kf, a5)


def _dft3_kernel(f_ref, b_ref, z_ref, x0_ref, d_ref, o_ref):
    f = f_ref[...]
    bt = jnp.swapaxes(b_ref[...], 0, 1)
    y = jnp.stack([_dft_dot(f, bt[s]) for s in range(bt.shape[0])], axis=0)
    y = jnp.swapaxes(y, 0, 1) + z_ref[...] * d_ref[...]
    o_ref[...] = (x0_ref[...] * y).astype(o_ref.dtype)


def _dft_stage3(f, b4, z3, x03, d_skip, block0=0):
    nb, rows2, n2, d = b4.shape
    r = f.shape[0]
    st = _tile(n2, FFT_ROWS, FFT_ROWS)
    dt = _tile(d, 512, LANE)
    vblk = pl.BlockSpec((r, st, dt), lambda b, i, j: (b + block0, i, j))
    return pl.pallas_call(
        _dft3_kernel,
        grid=(nb, n2 // st, d // dt),
        in_specs=[
            pl.BlockSpec((r, rows2), lambda b, i, j: (0, 0)),
            pl.BlockSpec((None, rows2, st, dt), lambda b, i, j: (b, 0, i, j)),
            vblk, vblk,
            pl.BlockSpec((1, dt), lambda b, i, j: (0, j)),
        ],
        out_specs=pl.BlockSpec((r, st, dt), lambda b, i, j: (b, i, j)),
        out_shape=jax.ShapeDtypeStruct((nb * r, n2, d), BF16),
        compiler_params=_params("parallel", "parallel", "parallel"),
        name="dft_stage3",
    )(f, b4, z3, x03, d_skip.reshape(1, d))


def _hyena_mix(hn, p, j, n_seq, seq_len, tables):
    d = hn.shape[1]
    n1, n2 = _fft_split(2 * seq_len)
    stage1, stage1_half, stage1_pair, fwd, inv, stage3_half, stage3_pair = tables
    u = _proj_bias(hn, (p['hy_w_in'], (j,)), p['hy_b_in'][j], jnp.ones((3 * d,), F32), F32)
    x0, z = _hy_pre(u, p['hy_conv_w'][j], n_seq, seq_len)
    dt = _tile(d, 4096, LANE)
    kfil = _hy_filter(seq_len, p['hy_f_w0'][j], p['hy_f_b0'][j], p['hy_f_w1'][j], p['hy_f_b1'][j],
                      p['hy_f_w2'][j], p['hy_f_b2'][j], p['hy_f_freq'][j], p['hy_f_out'][j])
    ka = _dft_stage1(stage1, kfil.reshape(n1, n2, d), 1)
    kf = _dft_stage2_filter(fwd, ka.reshape(2, n1, n2, d), dt)
    half = n1 // 2
    z3 = z.reshape(n_seq * half, n2, d)
    x03 = x0.reshape(n_seq * half, n2, d)
    outs = []
    n_pairs = n_seq // 2
    if n_pairs:
        a = _dft_stage1(stage1_pair, z3, n_pairs)
        b = _dft_stage2(fwd, inv, kf, a.reshape(n_pairs, 2, n1, n2, d), dt)
        outs.append(_dft_stage3(stage3_pair, b.reshape(n_pairs, 2 * n1, n2, d), z3, x03, p['hy_d'][j]))
    if n_seq % 2:
        a = _dft_stage1(stage1_half, z3, 1, block0=n_seq - 1)
        b = _dft_stage2(fwd, inv, kf, a.reshape(1, 2, n1, n2, d), dt)
        outs.append(_dft_stage3(stage3_half, b.reshape(1, 2 * n1, n2, d), z3, x03, p['hy_d'][j],
                                block0=n_seq - 1))
    out = outs[0] if len(outs) == 1 else jnp.concatenate(outs, axis=0)
    return out.reshape(n_seq * seq_len, d)


def _hyena_tables(seq_len):
    stage1, fwd, inv, stage3 = _dft_tables(2 * seq_len)
    n1 = stage1.shape[1]
    half = n1 // 2
    c, ms = stage1[:n1, :half], stage1[n1:, :half]
    stage1_pair = jnp.concatenate([jnp.concatenate([c, -ms], axis=1),
                                   jnp.concatenate([ms, c], axis=1)], axis=0)
    c3, ms3 = stage3[:half, :n1], stage3[:half, n1:]
    stage3_pair = jnp.concatenate([jnp.concatenate([c3, ms3], axis=1),
                                   jnp.concatenate([-ms3, c3], axis=1)], axis=0)
    return tuple(m.astype(BF16) for m in
                 (stage1, stage1[:, :half], stage1_pair, fwd, inv, stage3[:half], stage3_pair))


def _na_bias_table(rpb, rows):
    heads, nrh, nrw = rpb.shape
    kh_full, kw = (nrh + 1) // 2, (nrw + 1) // 2
    kh = min(kh_full, rows)
    w = GRID_W
    assert kw <= w and nrw <= 2 * w - 1
    cols = jnp.arange(w)
    col_start = jnp.clip(cols - kw // 2, 0, w - kw)
    valid = (cols[None, :] >= col_start[:, None]) & (cols[None, :] < col_start[:, None] + kw)
    u = jnp.pad(rpb, ((0, 0), (0, 0), (w - kw, 2 * w - (w - kw) - nrw)))
    flat = jnp.tile(u, (1, 1, w))[:, :, w - 1:w - 1 + w * (2 * w - 1)]
    toep = flat.reshape(heads, nrh, w, 2 * w - 1)[:, :, :, :w]
    toep = jnp.where(valid[None, None], toep, MASK_VALUE)
    tab = jnp.stack([toep[:, kh_full - 1 - case:kh_full - 1 - case + kh] for case in range(kh)], axis=1)
    tab = jnp.transpose(tab, (0, 1, 3, 2, 4))
    return tab.reshape(heads, kh, w, kh * w).astype(F32), kh


NA_CHUNK = 32
NA_UNROLL = 8
NA_MXU_UNROLL = 16


def _na_kernel(q_ref, k_ref, v_ref, bias_ref, o_ref, s_ref, p_ref, *, rows, kh):
    half = kh // 2
    chunk = s_ref.shape[0] // GRID_W
    keys = kh * GRID_W

    def chunk_body(c, carry):
        r0 = c * chunk

        def scores(i, carry):
            r = r0 + i
            rs = jnp.clip(r - half, 0, rows - kh)
            q = q_ref[pl.ds(pl.multiple_of(r * GRID_W, GRID_W), GRID_W), :]
            ks = k_ref[pl.ds(pl.multiple_of(rs * GRID_W, GRID_W), keys), :]
            s = lax.dot_general(q, ks, (((1,), (1,)), ((), ())), preferred_element_type=F32)
            s_ref[pl.ds(pl.multiple_of(i * GRID_W, GRID_W), GRID_W), :] = s + bias_ref[r - rs]
            return carry

        def softmax(i, carry):
            rows_i = pl.ds(pl.multiple_of(i * GRID_W, GRID_W), GRID_W)
            s = s_ref[rows_i, :]
            e = jnp.exp(s - jnp.max(s, axis=-1, keepdims=True))
            inv = 1.0 / jnp.sum(e, axis=-1, keepdims=True)
            p_ref[rows_i, :] = (e * inv).astype(p_ref.dtype)
            return carry

        def values(i, carry):
            r = r0 + i
            rs = jnp.clip(r - half, 0, rows - kh)
            vs = v_ref[pl.ds(pl.multiple_of(rs * GRID_W, GRID_W), keys), :]
            prob = p_ref[pl.ds(pl.multiple_of(i * GRID_W, GRID_W), GRID_W), :]
            o = jnp.dot(prob, vs, preferred_element_type=F32)
            o_ref[pl.ds(pl.multiple_of(r * GRID_W, GRID_W), GRID_W), :] = o.astype(o_ref.dtype)
            return carry

        lax.fori_loop(0, chunk, scores, 0, unroll=min(NA_MXU_UNROLL, chunk))
        lax.fori_loop(0, chunk, softmax, 0, unroll=NA_UNROLL)
        lax.fori_loop(0, chunk, values, 0, unroll=min(NA_MXU_UNROLL, chunk))
        return carry

    lax.fori_loop(0, rows // chunk, chunk_body, 0)


def _na_attention(qkv, rpb, n_seq, seq_len):
    t, d3 = qkv.shape
    d = d3 // 3
    heads = rpb.shape[0]
    hd = d // heads
    assert hd % LANE == 0 and seq_len % GRID_W == 0
    rows = seq_len // GRID_W
    bias, kh = _na_bias_table(rpb, rows)
    chunk = _tile(rows, NA_CHUNK, NA_MXU_UNROLL)
    blk = lambda part: pl.BlockSpec((seq_len, hd), lambda s, h: (s, part * heads + h))
    return pl.pallas_call(
        functools.partial(_na_kernel, rows=rows, kh=kh),
        grid=(n_seq, heads),
        in_specs=[blk(0), blk(1), blk(2),
                  pl.BlockSpec((None, kh, GRID_W, kh * GRID_W), lambda s, h: (h, 0, 0, 0))],
        out_specs=pl.BlockSpec((seq_len, hd), lambda s, h: (s, h)),
        out_shape=jax.ShapeDtypeStruct((t, d), BF16),
        scratch_shapes=[pltpu.VMEM((chunk * GRID_W, kh * GRID_W), F32),
                        pltpu.VMEM((chunk * GRID_W, kh * GRID_W), BF16)],
        compiler_params=_params("parallel", "parallel"),
        name="na_attention",
    )(qkv, qkv, qkv, bias)


def _na_mix(hn, p, j, n_seq, seq_len):
    d = hn.shape[1]
    heads = p['na_rpb'].shape[1]
    q_scale = (d // heads) ** -0.5
    col_scale = jnp.concatenate([jnp.full((d,), q_scale, F32), jnp.ones((2 * d,), F32)])
    qkv = _proj_bias(hn, (p['na_w_qkv'], (j,)), p['na_b_qkv'][j], col_scale, BF16)
    return _na_attention(qkv, p['na_rpb'][j], n_seq, seq_len)


def _ffn(x, hn_args, wg, wu, wd, gate, seq_len):
    hn = _norm_mod(x, *hn_args, seq_len)
    h = _swiglu_up(hn, wg, wu)
    zero_b = jnp.zeros((x.shape[1],), F32)
    return _proj_residual(h, wd, zero_b, 0.5 * gate, x, seq_len)


def kernel(x_prompt, x_sample, c_prompt, c_sample, norm_g, ada_w, ada_b, ffn_w_gate, ffn_w_up, ffn_w_down, hy_w_in, hy_b_in, hy_conv_w, hy_f_w0, hy_f_b0, hy_f_w1, hy_f_b1, hy_f_w2, hy_f_b2, hy_f_freq, hy_f_out, hy_d, hy_w_out, hy_b_out, na_w_qkv, na_b_qkv, na_rpb, na_w_out, na_b_out, final_g):
    nb_p, seq_len, d = x_prompt.shape
    nb_s = x_sample.shape[0]
    assert x_sample.shape[1:] == (seq_len, d)
    n_seq = nb_p + nb_s
    depth = ada_w.shape[0]
    assert ffn_w_gate.shape[-1] % LANE == 0

    x = jnp.concatenate([x_prompt.reshape(nb_p * seq_len, d), x_sample.reshape(nb_s * seq_len, d)], axis=0)
    c_act = jax.nn.silu(jnp.concatenate([c_prompt, c_sample], axis=0))
    mod = _ada_mod(c_act, ada_w, ada_b).reshape(depth, n_seq, ada_w.shape[-1] // d, d)

    wg_all = ffn_w_gate.astype(BF16)
    wu_all = ffn_w_up.astype(BF16)
    wd_all = ffn_w_down.astype(BF16)
    p = dict(hy_w_in=hy_w_in.astype(BF16), hy_b_in=hy_b_in, hy_conv_w=hy_conv_w,
             hy_f_w0=hy_f_w0, hy_f_b0=hy_f_b0, hy_f_w1=hy_f_w1, hy_f_b1=hy_f_b1,
             hy_f_w2=hy_f_w2, hy_f_b2=hy_f_b2, hy_f_freq=hy_f_freq, hy_f_out=hy_f_out,
             hy_d=hy_d, hy_w_out=hy_w_out.astype(BF16), hy_b_out=hy_b_out,
             na_w_qkv=na_w_qkv.astype(BF16), na_b_qkv=na_b_qkv, na_rpb=na_rpb,
             na_w_out=na_w_out.astype(BF16), na_b_out=na_b_out)

    tables = _hyena_tables(seq_len)
    for i in range(depth):
        m = mod[i]
        x = _ffn(x, (norm_g[i, 0], m[:, 1], m[:, 0]), (wg_all, (i, 0)), (wu_all, (i, 0)), (wd_all, (i, 0)), m[:, 2], seq_len)
        hn = _norm_mod(x, norm_g[i, 1], m[:, 4], m[:, 3], seq_len)
        j = i // 2
        if i % 2 == 0:
            mixed = _hyena_mix(hn, p, j, n_seq, seq_len, tables)
            x = _proj_residual(mixed, (p['hy_w_out'], (j,)), p['hy_b_out'][j], m[:, 5], x, seq_len)
        else:
            mixed = _na_mix(hn, p, j, n_seq, seq_len)
            x = _proj_residual(mixed, (p['na_w_out'], (j,)), p['na_b_out'][j], m[:, 5], x, seq_len)
        x = _ffn(x, (norm_g[i, 2], m[:, 7], m[:, 6]), (wg_all, (i, 1)), (wu_all, (i, 1)), (wd_all, (i, 1)), m[:, 8], seq_len)

    y_prompt = _final_norm(x, final_g, 0, nb_p * seq_len).reshape(nb_p, seq_len, d)
    y_sample = _final_norm(x, final_g, nb_p * seq_len, nb_s * seq_len).reshape(nb_s, seq_len, d)
    return (y_prompt, y_sample)
```

```python
import functools
import math

import jax
import jax.numpy as jnp
from jax import lax
from jax.experimental import pallas as pl
from jax.experimental.pallas import tpu as pltpu

F32 = jnp.float32
BF16 = jnp.bfloat16
HIGHEST = lax.Precision.HIGHEST

EPS = 1e-6
GRID_W = 64
HY_TARGET = 1e-2
HY_MAX_DECAY = math.log(HY_TARGET) / 0.3
HY_MIN_DECAY = math.log(HY_TARGET) / 1.5
MASK_VALUE = -1e30

LANE = 128
SUBLANE = 8
VMEM_LIMIT = 56 * 1024 * 1024


def _params(*sem):
    return pltpu.CompilerParams(dimension_semantics=sem, vmem_limit_bytes=VMEM_LIMIT)


def _tile(n, want, unit):
    if n <= want:
        return n
    t = (want // unit) * unit
    while t > unit and n % t:
        t -= unit
    assert n % t == 0, (n, want, unit)
    return t


ADA_ROWS = 64


def _ada_kernel(cb_ref, w_ref, b_ref, o_ref):
    n_seq = cb_ref.shape[0]
    k, tn = w_ref.shape
    nt = tn // LANE

    def fold(c, accs):
        rows = pl.ds(pl.multiple_of(c * ADA_ROWS, ADA_ROWS), ADA_ROWS)
        w = w_ref[rows, :]
        out = []
        for s in range(n_seq):
            cbs = cb_ref[s, rows, :]
            for t in range(nt):
                prod = w[:, t * LANE:(t + 1) * LANE] * cbs
                out.append(accs[s * nt + t] + prod.reshape(ADA_ROWS // SUBLANE, SUBLANE, LANE).sum(axis=0))
        return tuple(out)

    zero = jnp.zeros((SUBLANE, LANE), F32)
    accs = lax.fori_loop(0, k // ADA_ROWS, fold, (zero,) * (n_seq * nt))
    for s in range(n_seq):
        for t in range(nt):
            cols = slice(t * LANE, (t + 1) * LANE)
            o_ref[s:s + 1, cols] = jnp.sum(accs[s * nt + t], axis=0, keepdims=True) + b_ref[:, cols]


def _ada_mod(c_act, ada_w, ada_b):
    depth, k, n = ada_w.shape
    n_seq = c_act.shape[0]
    tn = _tile(n, 512, LANE)
    cb = jnp.broadcast_to(c_act[:, :, None], (n_seq, k, LANE))
    return pl.pallas_call(
        _ada_kernel,
        grid=(depth, n // tn),
        in_specs=[
            pl.BlockSpec((n_seq, k, LANE), lambda l, j: (0, 0, 0)),
            pl.BlockSpec((None, k, tn), lambda l, j: (l, 0, j)),
            pl.BlockSpec((None, 1, tn), lambda l, j: (l, 0, j)),
        ],
        out_specs=pl.BlockSpec((None, n_seq, tn), lambda l, j: (l, 0, j)),
        out_shape=jax.ShapeDtypeStruct((depth, n_seq, n), F32),
        compiler_params=_params("parallel", "parallel"),
        name="ada_mod",
    )(cb, ada_w, ada_b.reshape(depth, 1, n))


def _norm_mod_kernel(x_ref, g_ref, scale_ref, shift_ref, o_ref):
    x = x_ref[...]
    y = x * lax.rsqrt(jnp.mean(x * x, axis=-1, keepdims=True) + EPS)
    h = (y * g_ref[...]) * (1.0 + scale_ref[...]) + shift_ref[...]
    o_ref[...] = h.astype(o_ref.dtype)


def _norm_kernel(x_ref, g_ref, o_ref):
    x = x_ref[...]
    y = x * lax.rsqrt(jnp.mean(x * x, axis=-1, keepdims=True) + EPS)
    o_ref[...] = (y * g_ref[...]).astype(o_ref.dtype)


def _norm_mod(x, g, scale, shift, seq_len):
    t, d = x.shape
    bm = _tile(seq_len, 512, SUBLANE)
    per_seq = seq_len // bm
    vec = pl.BlockSpec((None, 1, d), lambda i: (i // per_seq, 0, 0))
    return pl.pallas_call(
        _norm_mod_kernel,
        grid=(t // bm,),
        in_specs=[
            pl.BlockSpec((bm, d), lambda i: (i, 0)),
            pl.BlockSpec((1, d), lambda i: (0, 0)),
            vec, vec,
        ],
        out_specs=pl.BlockSpec((bm, d), lambda i: (i, 0)),
        out_shape=jax.ShapeDtypeStruct((t, d), BF16),
        compiler_params=_params("parallel"),
        name="norm_mod",
    )(x, g.reshape(1, d), scale[:, None, :], shift[:, None, :])


def _final_norm(x, g, row0, rows):
    _, d = x.shape
    bm = _tile(math.gcd(rows, row0) if row0 else rows, 512, SUBLANE)
    off = row0 // bm
    return pl.pallas_call(
        _norm_kernel,
        grid=(rows // bm,),
        in_specs=[
            pl.BlockSpec((bm, d), lambda i: (i + off, 0)),
            pl.BlockSpec((1, d), lambda i: (0, 0)),
        ],
        out_specs=pl.BlockSpec((bm, d), lambda i: (i, 0)),
        out_shape=jax.ShapeDtypeStruct((rows, d), F32),
        compiler_params=_params("parallel"),
        name="final_norm",
    )(x, g.reshape(1, d))


def _wspec(wsel, block, index_fn):
    _, lead = wsel
    return pl.BlockSpec((None,) * len(lead) + block, lambda *g: lead + index_fn(*g))


SWIGLU_TILE_ELEMS = 1024 * 512


def _swiglu_kernel(a_ref, wg_ref, wu_ref, o_ref):
    a = a_ref[...]
    g = jnp.dot(a, wg_ref[...].astype(a.dtype), preferred_element_type=F32)
    u = jnp.dot(a, wu_ref[...].astype(a.dtype), preferred_element_type=F32)
    o_ref[...] = (g * jax.nn.sigmoid(g) * u).astype(o_ref.dtype)


def _swiglu_up(a, wg, wu):
    m, k = a.shape
    n = wg[0].shape[-1]
    bn = next(c for c in (512, 256, LANE) if n % c == 0)
    bm = _tile(m, SWIGLU_TILE_ELEMS // bn, SUBLANE)
    return pl.pallas_call(
        _swiglu_kernel,
        grid=(m // bm, n // bn),
        in_specs=[
            pl.BlockSpec((bm, k), lambda i, j: (i, 0)),
            _wspec(wg, (k, bn), lambda i, j: (0, j)),
            _wspec(wu, (k, bn), lambda i, j: (0, j)),
        ],
        out_specs=pl.BlockSpec((bm, bn), lambda i, j: (i, j)),
        out_shape=jax.ShapeDtypeStruct((m, n), BF16),
        compiler_params=_params("parallel", "arbitrary"),
        name="swiglu_up",
    )(a, wg[0], wu[0])


def _bias_kernel(a_ref, w_ref, b_ref, s_ref, o_ref):
    acc = jnp.dot(a_ref[...], w_ref[...], preferred_element_type=F32)
    o_ref[...] = ((acc + b_ref[...]) * s_ref[...]).astype(o_ref.dtype)


def _proj_bias(a, w, b, col_scale, out_dtype):
    m, k = a.shape
    n = w[0].shape[-1]
    bm = _tile(m, 1024, SUBLANE)
    bn = _tile(n, 1024, LANE)
    return pl.pallas_call(
        _bias_kernel,
        grid=(m // bm, n // bn),
        in_specs=[
            pl.BlockSpec((bm, k), lambda i, j: (i, 0)),
            _wspec(w, (k, bn), lambda i, j: (0, j)),
            pl.BlockSpec((1, bn), lambda i, j: (0, j)),
            pl.BlockSpec((1, bn), lambda i, j: (0, j)),
        ],
        out_specs=pl.BlockSpec((bm, bn), lambda i, j: (i, j)),
        out_shape=jax.ShapeDtypeStruct((m, n), out_dtype),
        compiler_params=_params("parallel", "arbitrary"),
        name="proj_bias",
    )(a, w[0], b.reshape(1, n), col_scale.reshape(1, n))


RESIDUAL_OPERAND_BYTES = 48 * 1024 * 1024


def _residual_kernel(a_ref, w_ref, b_ref, g_ref, x_ref, o_ref, *scratch, nk):
    if nk == 1:
        part = jnp.dot(a_ref[...], w_ref[...], preferred_element_type=F32)
        o_ref[...] = x_ref[...] + g_ref[...] * (part + b_ref[...])
        return
    acc_ref, = scratch
    kk = pl.program_id(2)

    @pl.when(kk == 0)
    def _():
        acc_ref[...] = jnp.zeros_like(acc_ref)

    acc_ref[...] += jnp.dot(a_ref[...], w_ref[...], preferred_element_type=F32)

    @pl.when(kk == nk - 1)
    def _():
        o_ref[...] = x_ref[...] + g_ref[...] * (acc_ref[...] + b_ref[...])


def _proj_residual(a, w, b, gate, x, seq_len):
    m, k = a.shape
    n = w[0].shape[-1]
    bn = _tile(n, 512, LANE)
    bm = _tile(seq_len, 1024, SUBLANE)
    while bm > 256 and 2 * 2 * k * (bm + bn) > RESIDUAL_OPERAND_BYTES:
        bm //= 2
    bk = k if 2 * 2 * k * (bm + bn) <= RESIDUAL_OPERAND_BYTES else _tile(k, 4096, LANE)
    nk = k // bk
    per_seq = seq_len // bm
    return pl.pallas_call(
        functools.partial(_residual_kernel, nk=nk),
        grid=(m // bm, n // bn, nk),
        in_specs=[
            pl.BlockSpec((bm, bk), lambda i, j, kk: (i, kk)),
            _wspec(w, (bk, bn), lambda i, j, kk: (kk, j)),
            pl.BlockSpec((1, bn), lambda i, j, kk: (0, j)),
            pl.BlockSpec((None, 1, bn), lambda i, j, kk: (i // per_seq, 0, j)),
            pl.BlockSpec((bm, bn), lambda i, j, kk: (i, j)),
        ],
        out_specs=pl.BlockSpec((bm, bn), lambda i, j, kk: (i, j)),
        out_shape=jax.ShapeDtypeStruct((m, n), F32),
        scratch_shapes=[pltpu.VMEM((bm, bn), F32)] if nk > 1 else [],
        input_output_aliases={4: 0},
        compiler_params=_params("parallel", "parallel", "arbitrary"),
        name="proj_residual",
    )(a, w[0], b.reshape(1, n), gate[:, None, :], x)


def _hy_pre_kernel(u0_ref, u1_ref, uv_ref, c0_ref, c1_ref, cv_ref, x0_ref, z_ref):
    rows = u0_ref.shape[0]
    ridx = lax.broadcasted_iota(jnp.int32, u0_ref.shape, 0)
    first = ridx == 0
    last = ridx == rows - 1

    def conv3(u_ref, c_ref):
        u = u_ref[...]
        prev = jnp.where(first, 0.0, pltpu.roll(u, 1, 0))
        nxt = jnp.where(last, 0.0, pltpu.roll(u, rows - 1, 0))
        return prev * c_ref[0:1, :] + u * c_ref[1:2, :] + nxt * c_ref[2:3, :]

    x0_ref[...] = conv3(u0_ref, c0_ref)
    z_ref[...] = conv3(uv_ref, cv_ref) * conv3(u1_ref, c1_ref)


def _hy_pre(u, conv_w, n_seq, seq_len):
    t, d3 = u.shape
    d = d3 // 3
    dt = LANE
    nd = d // dt
    ublk = lambda part: pl.BlockSpec((seq_len, dt), lambda s, j: (s, part * nd + j))
    cblk = lambda part: pl.BlockSpec((3, dt), lambda s, j: (0, part * nd + j))
    oblk = pl.BlockSpec((seq_len, dt), lambda s, j: (s, j))
    return pl.pallas_call(
        _hy_pre_kernel,
        grid=(n_seq, nd),
        in_specs=[ublk(0), ublk(1), ublk(2), cblk(0), cblk(1), cblk(2)],
        out_specs=[oblk, oblk],
        out_shape=[jax.ShapeDtypeStruct((t, d), F32), jax.ShapeDtypeStruct((t, d), F32)],
        compiler_params=_params("parallel", "parallel"),
        name="hy_pre",
    )(u, u, u, conv_w, conv_w, conv_w)


def _hy_filter_kernel(w0_ref, b0_ref, w1_ref, b1_ref, w2_ref, b2_ref, fr_ref, wo_ref, k_ref,
                      *, seq_len, bands, d_model):
    rows = k_ref.shape[0]
    n = 2 * seq_len
    r = pl.program_id(0) * rows + lax.broadcasted_iota(jnp.int32, (rows, 1), 0)
    m = jnp.where(r < seq_len, r, n - r).astype(F32)
    t = m * (1.0 / (seq_len - 1.0))
    w = (2.0 * math.pi / seq_len) * m
    lane = lax.broadcasted_iota(jnp.int32, (1, LANE), 1)
    band = jnp.where(lane <= bands, lane - 1, lane - 1 - bands).astype(F32)
    f = 1e-4 + band * ((bands - 1 - 1e-4) / (bands - 1))
    ang = w * f
    feat = jnp.where(lane == 0, t,
                     jnp.where(lane <= bands, jnp.cos(ang),
                               jnp.where(lane <= 2 * bands, -jnp.sin(ang), 0.0)))
    fr = fr_ref[...]
    h = jnp.sin(fr * (jnp.dot(feat, w0_ref[...], precision=HIGHEST) + b0_ref[...]))
    h = jnp.sin(fr * (jnp.dot(h, w1_ref[...], precision=HIGHEST) + b1_ref[...]))
    h = jnp.sin(fr * (jnp.dot(h, w2_ref[...], precision=HIGHEST) + b2_ref[...]))
    taps = jnp.dot(h, wo_ref[...], precision=HIGHEST)
    ch = lax.broadcasted_iota(jnp.int32, (1, d_model), 1).astype(F32)
    deltas = jnp.abs(HY_MIN_DECAY + ch * ((HY_MAX_DECAY - HY_MIN_DECAY) / (d_model - 1)))
    k = taps * jnp.exp(-t * deltas)
    k_ref[...] = jnp.where(r == seq_len, 0.0, k)


def _hy_filter(seq_len, f_w0, f_b0, f_w1, f_b1, f_w2, f_b2, f_freq, f_out):
    emb, hid = f_w0.shape
    d = f_out.shape[1] // 2
    bands = (emb - 1) // 2
    assert emb <= LANE and hid <= LANE

    def pad2(a, r, c):
        return jnp.pad(a, ((0, r - a.shape[0]), (0, c - a.shape[1])))

    vec = lambda a: pad2(a.reshape(1, -1), 1, LANE)
    rows = _tile(seq_len, 512, SUBLANE)
    nfwd = seq_len // rows
    full = lambda shape: pl.BlockSpec(shape, lambda i: (0, 0))
    return pl.pallas_call(
        functools.partial(_hy_filter_kernel, seq_len=seq_len, bands=bands, d_model=d),
        grid=(2 * nfwd,),
        in_specs=[
            full((LANE, LANE)), full((1, LANE)),
            full((LANE, LANE)), full((1, LANE)),
            full((LANE, LANE)), full((1, LANE)),
            full((1, LANE)),
            pl.BlockSpec((LANE, d), lambda i: (0, i // nfwd)),
        ],
        out_specs=pl.BlockSpec((rows, d), lambda i: (i, 0)),
        out_shape=jax.ShapeDtypeStruct((2 * seq_len, d), F32),
        compiler_params=_params("parallel"),
        name="hy_filter",
    )(pad2(f_w0, LANE, LANE), vec(f_b0), pad2(f_w1, LANE, LANE), vec(f_b1),
      pad2(f_w2, LANE, LANE), vec(f_b2), vec(f_freq), pad2(f_out, LANE, 2 * d))


def _fft_split(n):
    lg = n.bit_length() - 1
    assert n == 1 << lg
    n1 = 1 << (lg // 2)
    return n1, n // n1


def _dft_tables(n):
    n1, n2 = _fft_split(n)

    def cs(phase, period):
        ang = (2.0 * math.pi / period) * (phase % period).astype(F32)
        return jnp.cos(ang), jnp.sin(ang)

    i1 = jnp.arange(n1, dtype=jnp.int32)
    c1, s1 = cs(i1[:, None] * i1[None, :], n1)
    stage1 = jnp.concatenate([c1, -s1], axis=0)
    stage3 = jnp.concatenate([c1, -s1], axis=1) * (1.0 / n)
    i2 = jnp.arange(n2, dtype=jnp.int32)
    tc, ts = cs(i1[:, None] * i2[None, :], n)
    fc, fs = cs(i2[:, None] * i2[None, :], n2)
    gc = tc[:, None, :] * fc[None] - ts[:, None, :] * fs[None]
    gs = ts[:, None, :] * fc[None] + tc[:, None, :] * fs[None]
    fwd = jnp.concatenate([jnp.concatenate([gc, gs], axis=2),
                           jnp.concatenate([-gs, gc], axis=2)], axis=1)
    inv = jnp.swapaxes(fwd, 1, 2)
    return stage1, fwd, inv, stage3


def _dft_dot(f, x):
    return jnp.dot(f, x.astype(BF16), preferred_element_type=F32)


FFT_ROWS = 16


def _dft1_kernel(f_ref, z_ref, o_ref):
    f = f_ref[...]
    zt = jnp.swapaxes(z_ref[...].astype(BF16), 0, 1)
    out = jnp.stack([_dft_dot(f, zt[s]) for s in range(zt.shape[0])], axis=0)
    o_ref[...] = jnp.swapaxes(out.astype(o_ref.dtype), 0, 1)


def _dft_stage1(f, z3, nb, block0=0):
    _, n2, d = z3.shape
    m, r = f.shape
    st = _tile(n2, FFT_ROWS, FFT_ROWS)
    dt = _tile(d, 512, LANE)
    return pl.pallas_call(
        _dft1_kernel,
        grid=(nb, n2 // st, d // dt),
        in_specs=[
            pl.BlockSpec((m, r), lambda b, i, j: (0, 0)),
            pl.BlockSpec((r, st, dt), lambda b, i, j: (b + block0, i, j)),
        ],
        out_specs=pl.BlockSpec((None, m, st, dt), lambda b, i, j: (b, 0, i, j)),
        out_shape=jax.ShapeDtypeStruct((nb, m, n2, d), BF16),
        compiler_params=_params("parallel", "parallel", "parallel"),
        name="dft_stage1",
    )(f, z3)


def _dft2_filter_kernel(mf_ref, a_ref, o_ref):
    n2 = a_ref.shape[1]
    a = a_ref[...].reshape(2 * n2, a_ref.shape[2])
    o_ref[...] = _dft_dot(mf_ref[...], a).reshape(o_ref.shape)


def _dft_stage2_filter(fwd, a5, dt):
    _, n1, n2, d = a5.shape
    return pl.pallas_call(
        _dft2_filter_kernel,
        grid=(n1, d // dt),
        in_specs=[
            pl.BlockSpec((None, 2 * n2, 2 * n2), lambda k, j: (k, 0, 0)),
            pl.BlockSpec((2, None, n2, dt), lambda k, j: (0, k, 0, j)),
        ],
        out_specs=pl.BlockSpec((2, None, n2, dt), lambda k, j: (0, k, 0, j)),
        out_shape=jax.ShapeDtypeStruct(a5.shape, F32),
        compiler_params=_params("parallel", "parallel"),
        name="dft_stage2_filter",
    )(fwd, a5)


def _dft2_kernel(mf_ref, mi_ref, kf_ref, a_ref, o_ref):
    n2 = a_ref.shape[1]
    a = a_ref[...].reshape(2 * n2, a_ref.shape[2])
    x = _dft_dot(mf_ref[...], a)
    xr, xi = x[:n2], x[n2:]
    kr, ki = kf_ref[0], kf_ref[1]
    y = jnp.concatenate([xr * kr - xi * ki, xr * ki + xi * kr], axis=0)
    o_ref[...] = _dft_dot(mi_ref[...], y).astype(o_ref.dtype).reshape(o_ref.shape)


def _dft_stage2(fwd, inv, kf, a5, dt):
    nb, _, n1, n2, d = a5.shape
    ablk = pl.BlockSpec((None, 2, None, n2, dt), lambda k, j, b: (b, 0, k, 0, j))
    mblk = pl.BlockSpec((None, 2 * n2, 2 * n2), lambda k, j, b: (k, 0, 0))
    return pl.pallas_call(
        _dft2_kernel,
        grid=(n1, d // dt, nb),
        in_specs=[
            mblk, mblk,
            pl.BlockSpec((2, None, n2, dt), lambda k, j, b: (0, k, 0, j)),
            ablk,
        ],
        out_specs=ablk,
        out_shape=jax.ShapeDtypeStruct(a5.shape, BF16),
        compiler_params=_params("parallel", "parallel", "parallel"),
        name="dft_stage2",
    )(fwd, inv, kf, a5)


def _dft3_kernel(f_ref, b_ref, z_ref, x0_ref, d_ref, o_ref):
    f = f_ref[...]
    bt = jnp.swapaxes(b_ref[...], 0, 1)
    y = jnp.stack([_dft_dot(f, bt[s]) for s in range(bt.shape[0])], axis=0)
    y = jnp.swapaxes(y, 0, 1) + z_ref[...] * d_ref[...]
    o_ref[...] = (x0_ref[...] * y).astype(o_ref.dtype)


def _dft_stage3(f, b4, z3, x03, d_skip, block0=0):
    nb, rows2, n2, d = b4.shape
    r = f.shape[0]
    st = _tile(n2, FFT_ROWS, FFT_ROWS)
    dt = _tile(d, 512, LANE)
    vblk = pl.BlockSpec((r, st, dt), lambda b, i, j: (b + block0, i, j))
    return pl.pallas_call(
        _dft3_kernel,
        grid=(nb, n2 // st, d // dt),
        in_specs=[
            pl.BlockSpec((r, rows2), lambda b, i, j: (0, 0)),
            pl.BlockSpec((None, rows2, st, dt), lambda b, i, j: (b, 0, i, j)),
            vblk, vblk,
            pl.BlockSpec((1, dt), lambda b, i, j: (0, j)),
        ],
        out_specs=pl.BlockSpec((r, st, dt), lambda b, i, j: (b, i, j)),
        out_shape=jax.ShapeDtypeStruct((nb * r, n2, d), BF16),
        compiler_params=_params("parallel", "parallel", "parallel"),
        name="dft_stage3",
    )(f, b4, z3, x03, d_skip.reshape(1, d))


def _hyena_mix(hn, p, j, n_seq, seq_len, tables):
    d = hn.shape[1]
    n1, n2 = _fft_split(2 * seq_len)
    stage1, stage1_half, stage1_pair, fwd, inv, stage3_half, stage3_pair = tables
    u = _proj_bias(hn, (p['hy_w_in'], (j,)), p['hy_b_in'][j], jnp.ones((3 * d,), F32), F32)
    x0, z = _hy_pre(u, p['hy_conv_w'][j], n_seq, seq_len)
    dt = _tile(d, 4096, LANE)
    kfil = _hy_filter(seq_len, p['hy_f_w0'][j], p['hy_f_b0'][j], p['hy_f_w1'][j], p['hy_f_b1'][j],
                      p['hy_f_w2'][j], p['hy_f_b2'][j], p['hy_f_freq'][j], p['hy_f_out'][j])
    ka = _dft_stage1(stage1, kfil.reshape(n1, n2, d), 1)
    kf = _dft_stage2_filter(fwd, ka.reshape(2, n1, n2, d), dt)
    half = n1 // 2
    z3 = z.reshape(n_seq * half, n2, d)
    x03 = x0.reshape(n_seq * half, n2, d)
    outs = []
    n_pairs = n_seq // 2
    if n_pairs:
        a = _dft_stage1(stage1_pair, z3, n_pairs)
        b = _dft_stage2(fwd, inv, kf, a.reshape(n_pairs, 2, n1, n2, d), dt)
        outs.append(_dft_stage3(stage3_pair, b.reshape(n_pairs, 2 * n1, n2, d), z3, x03, p['hy_d'][j]))
    if n_seq % 2:
        a = _dft_stage1(stage1_half, z3, 1, block0=n_seq - 1)
        b = _dft_stage2(fwd, inv, kf, a.reshape(1, 2, n1, n2, d), dt)
        outs.append(_dft_stage3(stage3_half, b.reshape(1, 2 * n1, n2, d), z3, x03, p['hy_d'][j],
                                block0=n_seq - 1))
    out = outs[0] if len(outs) == 1 else jnp.concatenate(outs, axis=0)
    return out.reshape(n_seq * seq_len, d)


def _hyena_tables(seq_len):
    stage1, fwd, inv, stage3 = _dft_tables(2 * seq_len)
    n1 = stage1.shape[1]
    half = n1 // 2
    c, ms = stage1[:n1, :half], stage1[n1:, :half]
    stage1_pair = jnp.concatenate([jnp.concatenate([c, -ms], axis=1),
                                   jnp.concatenate([ms, c], axis=1)], axis=0)
    c3, ms3 = stage3[:half, :n1], stage3[:half, n1:]
    stage3_pair = jnp.concatenate([jnp.concatenate([c3, ms3], axis=1),
                                   jnp.concatenate([-ms3, c3], axis=1)], axis=0)
    return tuple(m.astype(BF16) for m in
                 (stage1, stage1[:, :half], stage1_pair, fwd, inv, stage3[:half], stage3_pair))


def _na_bias_table(rpb, rows):
    heads, nrh, nrw = rpb.shape
    kh_full, kw = (nrh + 1) // 2, (nrw + 1) // 2
    kh = min(kh_full, rows)
    w = GRID_W
    assert kw <= w and nrw <= 2 * w - 1
    cols = jnp.arange(w)
    col_start = jnp.clip(cols - kw // 2, 0, w - kw)
    valid = (cols[None, :] >= col_start[:, None]) & (cols[None, :] < col_start[:, None] + kw)
    u = jnp.pad(rpb, ((0, 0), (0, 0), (w - kw, 2 * w - (w - kw) - nrw)))
    flat = jnp.tile(u, (1, 1, w))[:, :, w - 1:w - 1 + w * (2 * w - 1)]
    toep = flat.reshape(heads, nrh, w, 2 * w - 1)[:, :, :, :w]
    toep = jnp.where(valid[None, None], toep, MASK_VALUE)
    tab = jnp.stack([toep[:, kh_full - 1 - case:kh_full - 1 - case + kh] for case in range(kh)], axis=1)
    tab = jnp.transpose(tab, (0, 1, 3, 2, 4))
    return tab.reshape(heads, kh, w, kh * w).astype(F32), kh


NA_CHUNK = 32
NA_UNROLL = 8
NA_MXU_UNROLL = 16


def _na_kernel(q_ref, k_ref, v_ref, bias_ref, o_ref, s_ref, p_ref, *, rows, kh):
    half = kh // 2
    chunk = s_ref.shape[0] // GRID_W
    keys = kh * GRID_W

    def chunk_body(c, carry):
        r0 = c * chunk

        def scores(i, carry):
            r = r0 + i
            rs = jnp.clip(r - half, 0, rows - kh)
            q = q_ref[pl.ds(pl.multiple_of(r * GRID_W, GRID_W), GRID_W), :]
            ks = k_ref[pl.ds(pl.multiple_of(rs * GRID_W, GRID_W), keys), :]
            s = lax.dot_general(q, ks, (((1,), (1,)), ((), ())), preferred_element_type=F32)
            s_ref[pl.ds(pl.multiple_of(i * GRID_W, GRID_W), GRID_W), :] = s + bias_ref[r - rs]
            return carry

        def softmax(i, carry):
            rows_i = pl.ds(pl.multiple_of(i * GRID_W, GRID_W), GRID_W)
            s = s_ref[rows_i, :]
            e = jnp.exp(s - jnp.max(s, axis=-1, keepdims=True))
            inv = 1.0 / jnp.sum(e, axis=-1, keepdims=True)
            p_ref[rows_i, :] = (e * inv).astype(p_ref.dtype)
            return carry

        def values(i, carry):
            r = r0 + i
            rs = jnp.clip(r - half, 0, rows - kh)
            vs = v_ref[pl.ds(pl.multiple_of(rs * GRID_W, GRID_W), keys), :]
            prob = p_ref[pl.ds(pl.multiple_of(i * GRID_W, GRID_W), GRID_W), :]
            o = jnp.dot(prob, vs, preferred_element_type=F32)
            o_ref[pl.ds(pl.multiple_of(r * GRID_W, GRID_W), GRID_W), :] = o.astype(o_ref.dtype)
            return carry

        lax.fori_loop(0, chunk, scores, 0, unroll=min(NA_MXU_UNROLL, chunk))
        lax.fori_loop(0, chunk, softmax, 0, unroll=NA_UNROLL)
        lax.fori_loop(0, chunk, values, 0, unroll=min(NA_MXU_UNROLL, chunk))
        return carry

    lax.fori_loop(0, rows // chunk, chunk_body, 0)


def _na_attention(qkv, rpb, n_seq, seq_len):
    t, d3 = qkv.shape
    d = d3 // 3
    heads = rpb.shape[0]
    hd = d // heads
    assert hd % LANE == 0 and seq_len % GRID_W == 0
    rows = seq_len // GRID_W
    bias, kh = _na_bias_table(rpb, rows)
    chunk = _tile(rows, NA_CHUNK, NA_MXU_UNROLL)
    blk = lambda part: pl.BlockSpec((seq_len, hd), lambda s, h: (s, part * heads + h))
    return pl.pallas_call(
        functools.partial(_na_kernel, rows=rows, kh=kh),
        grid=(n_seq, heads),
        in_specs=[blk(0), blk(1), blk(2),
                  pl.BlockSpec((None, kh, GRID_W, kh * GRID_W), lambda s, h: (h, 0, 0, 0))],
        out_specs=pl.BlockSpec((seq_len, hd), lambda s, h: (s, h)),
        out_shape=jax.ShapeDtypeStruct((t, d), BF16),
        scratch_shapes=[pltpu.VMEM((chunk * GRID_W, kh * GRID_W), F32),
                        pltpu.VMEM((chunk * GRID_W, kh * GRID_W), BF16)],
        compiler_params=_params("parallel", "parallel"),
        name="na_attention",
    )(qkv, qkv, qkv, bias)


def _na_mix(hn, p, j, n_seq, seq_len):
    d = hn.shape[1]
    heads = p['na_rpb'].shape[1]
    q_scale = (d // heads) ** -0.5
    col_scale = jnp.concatenate([jnp.full((d,), q_scale, F32), jnp.ones((2 * d,), F32)])
    qkv = _proj_bias(hn, (p['na_w_qkv'], (j,)), p['na_b_qkv'][j], col_scale, BF16)
    return _na_attention(qkv, p['na_rpb'][j], n_seq, seq_len)


def _ffn(x, hn_args, wg, wu, wd, gate, seq_len):
    hn = _norm_mod(x, *hn_args, seq_len)
    h = _swiglu_up(hn, wg, wu)
    zero_b = jnp.zeros((x.shape[1],), F32)
    return _proj_residual(h, wd, zero_b, 0.5 * gate, x, seq_len)


def kernel(x_prompt, x_sample, c_prompt, c_sample, norm_g, ada_w, ada_b, ffn_w_gate, ffn_w_up, ffn_w_down, hy_w_in, hy_b_in, hy_conv_w, hy_f_w0, hy_f_b0, hy_f_w1, hy_f_b1, hy_f_w2, hy_f_b2, hy_f_freq, hy_f_out, hy_d, hy_w_out, hy_b_out, na_w_qkv, na_b_qkv, na_rpb, na_w_out, na_b_out, final_g):
    nb_p, seq_len, d = x_prompt.shape
    nb_s = x_sample.shape[0]
    assert x_sample.shape[1:] == (seq_len, d)
    n_seq = nb_p + nb_s
    depth = ada_w.shape[0]
    assert ffn_w_gate.shape[-1] % LANE == 0

    x = jnp.concatenate([x_prompt.reshape(nb_p * seq_len, d), x_sample.reshape(nb_s * seq_len, d)], axis=0)
    c_act = jax.nn.silu(jnp.concatenate([c_prompt, c_sample], axis=0))
    mod = _ada_mod(c_act, ada_w, ada_b).reshape(depth, n_seq, ada_w.shape[-1] // d, d)

    wg_all = ffn_w_gate
    wu_all = ffn_w_up
    wd_all = ffn_w_down.astype(BF16)
    p = dict(hy_w_in=hy_w_in.astype(BF16), hy_b_in=hy_b_in, hy_conv_w=hy_conv_w,
             hy_f_w0=hy_f_w0, hy_f_b0=hy_f_b0, hy_f_w1=hy_f_w1, hy_f_b1=hy_f_b1,
             hy_f_w2=hy_f_w2, hy_f_b2=hy_f_b2, hy_f_freq=hy_f_freq, hy_f_out=hy_f_out,
             hy_d=hy_d, hy_w_out=hy_w_out.astype(BF16), hy_b_out=hy_b_out,
             na_w_qkv=na_w_qkv.astype(BF16), na_b_qkv=na_b_qkv, na_rpb=na_rpb,
             na_w_out=na_w_out.astype(BF16), na_b_out=na_b_out)

    tables = _hyena_tables(seq_len)
    for i in range(depth):
        m = mod[i]
        x = _ffn(x, (norm_g[i, 0], m[:, 1], m[:, 0]), (wg_all, (i, 0)), (wu_all, (i, 0)), (wd_all, (i, 0)), m[:, 2], seq_len)
        hn = _norm_mod(x, norm_g[i, 1], m[:, 4], m[:, 3], seq_len)
        j = i // 2
        if i % 2 == 0:
            mixed = _hyena_mix(hn, p, j, n_seq, seq_len, tables)
            x = _proj_residual(mixed, (p['hy_w_out'], (j,)), p['hy_b_out'][j], m[:, 5], x, seq_len)
        else:
            mixed = _na_mix(hn, p, j, n_seq, seq_len)
            x = _proj_residual(mixed, (p['na_w_out'], (j,)), p['na_b_out'][j], m[:, 5], x, seq_len)
        x = _ffn(x, (norm_g[i, 2], m[:, 7], m[:, 6]), (wg_all, (i, 1)), (wu_all, (i, 1)), (wd_all, (i, 1)), m[:, 8], seq_len)

    y_prompt = _final_norm(x, final_g, 0, nb_p * seq_len).reshape(nb_p, seq_len, d)
    y_sample = _final_norm(x, final_g, nb_p * seq_len, nb_s * seq_len).reshape(nb_s, seq_len, d)
    return (y_prompt, y_sample)
```
